```python
import jax, jax.numpy as jnp
from jax import lax
import numpy as np


D_MODEL = 2048
BATCH = 4
SEQ = 4096
DEPTH = 2

HEAD_DIM = 128
CONV_CH = 1024
CONV_GROUPS = CONV_CH // HEAD_DIM
CONV_WIDTH = 31
MOBA_HEADS = 8
MOBA_DIM = MOBA_HEADS * HEAD_DIM
MOBA_BLOCK = 256
MOBA_TOPK = 3
MOBA_Q_CHUNK = 32
SB_HEADS = 16
SB_DIM = SB_HEADS * HEAD_DIM
SB_Q_BLOCK = 128
D_FF = 5632
FFN_CONV_WIDTH = 3
RMS_EPS = 1e-6
LN_EPS = 1e-5
EVEN_IN = 2 * CONV_CH + 3 * MOBA_DIM
EVEN_MIX = CONV_CH + MOBA_DIM
N_EVEN = (DEPTH + 1) // 2
N_ODD = DEPTH // 2

kernel_name = "hybrid_conformer_moba_stickbreak_block"


def rms_norm(x, g):
    xf = x.astype(jnp.float32)
    y = xf * lax.rsqrt(jnp.mean(xf * xf, axis=-1, keepdims=True) + RMS_EPS)
    return y.astype(x.dtype) * g


def causal_depthwise_conv(x, w, b):
    width = w.shape[0]
    xp = jnp.pad(x, ((0, 0), (width - 1, 0), (0, 0)))
    y = lax.conv_general_dilated(xp, w[:, None, :].astype(x.dtype), window_strides=(1,), padding='VALID',
                                 dimension_numbers=('NWC', 'WIO', 'NWC'), feature_group_count=x.shape[-1])
    return y + b.astype(x.dtype)


def conformer_conv(u, w_dw, b_dw, ln_g, ln_b):
    a, g = jnp.split(u, 2, axis=-1)
    h = causal_depthwise_conv(a * jax.nn.sigmoid(g), w_dw, b_dw)
    hf = h.astype(jnp.float32)
    mu = jnp.mean(hf, axis=-1, keepdims=True)
    var = jnp.mean(jnp.square(hf - mu), axis=-1, keepdims=True)
    h = ((hf - mu) * lax.rsqrt(var + LN_EPS)).astype(h.dtype) * ln_g + ln_b
    return jax.nn.silu(h)


def moba_attention(q, k, v):
    bsz, nh, t_len, dh = q.shape
    nb = -(-t_len // MOBA_BLOCK)
    pad = ((0, 0), (0, 0), (0, nb * MOBA_BLOCK - t_len), (0, 0))
    kb = jnp.pad(k, pad).reshape(bsz, nh, nb, MOBA_BLOCK, dh)
    vb = jnp.pad(v, pad).reshape(bsz, nh, nb, MOBA_BLOCK, dh)
    kmean = jnp.mean(kb.astype(jnp.float32), axis=3).astype(k.dtype)
    topk = min(MOBA_TOPK, nb)
    scale = dh ** -0.5
    n_chunks = t_len // MOBA_Q_CHUNK
    qc = q.reshape(bsz, nh, n_chunks, MOBA_Q_CHUNK, dh).transpose(2, 0, 1, 3, 4)
    gather = jax.vmap(jax.vmap(lambda blocks, ix: blocks[ix]))
    blk_ids = jnp.arange(nb)
    slot_ids = jnp.arange(topk)

    def chunk(args):
        ci, qi = args
        t0 = ci * MOBA_Q_CHUNK
        own = t0 // MOBA_BLOCK
        qpos = t0 + jnp.arange(MOBA_Q_CHUNK)
        gate = jnp.einsum('bhqd,bhnd->bhqn', qi, kmean).astype(jnp.float32)
        gate = jnp.where(blk_ids < own, gate, -jnp.inf)
        _, idx = lax.top_k(gate, topk)
        slot_ok = slot_ids < own
        ksel = gather(kb, idx)
        vsel = gather(vb, idx)
        s_sel = jnp.einsum('bhqd,bhqjsd->bhqjs', qi, ksel).astype(jnp.float32) * scale
        s_sel = jnp.where(slot_ok[:, None], s_sel, -jnp.inf).reshape(bsz, nh, MOBA_Q_CHUNK, topk * MOBA_BLOCK)
        kown = lax.dynamic_index_in_dim(kb, own, axis=2, keepdims=False)
        vown = lax.dynamic_index_in_dim(vb, own, axis=2, keepdims=False)
        kpos = own * MOBA_BLOCK + jnp.arange(MOBA_BLOCK)
        s_own = jnp.einsum('bhqd,bhsd->bhqs', qi, kown).astype(jnp.float32) * scale
        s_own = jnp.where(kpos[None, :] <= qpos[:, None], s_own, -jnp.inf)
        p = jax.nn.softmax(jnp.concatenate([s_sel, s_own], axis=-1), axis=-1).astype(v.dtype)
        p_sel = p[..., :topk * MOBA_BLOCK].reshape(bsz, nh, MOBA_Q_CHUNK, topk, MOBA_BLOCK)
        p_own = p[..., topk * MOBA_BLOCK:]
        return (jnp.einsum('bhqjs,bhqjsd->bhqd', p_sel, vsel)
                + jnp.einsum('bhqs,bhsd->bhqd', p_own, vown))

    out = lax.map(chunk, (jnp.arange(n_chunks), qc))
    return out.transpose(1, 2, 0, 3, 4).reshape(bsz, nh, t_len, dh)


def stick_breaking_attention(q, k, v):
    bsz, nh, t_len, dh = q.shape
    nblk = t_len // SB_Q_BLOCK
    qb = q.reshape(bsz, nh, nblk, SB_Q_BLOCK, dh).transpose(2, 0, 1, 3, 4)
    kpos = jnp.arange(t_len)
    scale = dh ** -0.5

    def block(args):
        bi, qi = args
        qpos = bi * SB_Q_BLOCK + jnp.arange(SB_Q_BLOCK)
        z = jnp.einsum('bhqd,bhsd->bhqs', qi, k).astype(jnp.float32) * scale
        strict = kpos[None, :] < qpos[:, None]
        log_1m = jnp.where(strict, jax.nn.log_sigmoid(-z), 0.0)
        between = lax.cumsum(log_1m, axis=log_1m.ndim - 1, reverse=True) - log_1m
        a = jnp.where(strict, jnp.exp(jax.nn.log_sigmoid(z) + between), 0.0).astype(v.dtype)
        return jnp.einsum('bhqs,bhsd->bhqd', a, v)

    out = lax.map(block, (jnp.arange(nblk), qb))
    return out.transpose(1, 2, 0, 3, 4).reshape(bsz, nh, t_len, dh)


def split_heads(t, n_heads):
    b, s, _ = t.shape
    return t.reshape(b, s, n_heads, HEAD_DIM).transpose(0, 2, 1, 3)


def merge_heads(t):
    b, h, s, d = t.shape
    return t.transpose(0, 2, 1, 3).reshape(b, s, h * d)


def even_mixer(h, w_in, conv_w, conv_b, ln_g, ln_b, w_out):
    proj = h @ w_in
    u_conv = proj[..., :2 * CONV_CH]
    q, k, v = jnp.split(proj[..., 2 * CONV_CH:], 3, axis=-1)
    y_a = conformer_conv(u_conv, conv_w, conv_b, ln_g, ln_b)
    y_b = merge_heads(moba_attention(split_heads(q, MOBA_HEADS), split_heads(k, MOBA_HEADS),
                                     split_heads(v, MOBA_HEADS)))
    return jnp.concatenate([y_a, y_b], axis=-1) @ w_out


def odd_mixer(h, w_qkv, w_o):
    q, k, v = jnp.split(h @ w_qkv, 3, axis=-1)
    y = stick_breaking_attention(split_heads(q, SB_HEADS), split_heads(k, SB_HEADS), split_heads(v, SB_HEADS))
    return merge_heads(y) @ w_o


def conv_glu_ffn(h, w_up, w_gate, conv_w, conv_b, w_down):
    u = causal_depthwise_conv(h @ w_up, conv_w, conv_b)
    return (jax.nn.silu(u) * (h @ w_gate)) @ w_down


def setup_inputs(seed: int = 0) -> dict:
    key = jax.random.key(seed)
    ks = jax.random.split(key, 20)
    f32 = jnp.float32
    nrm = lambda k, shape, s: jax.random.normal(k, shape, f32) * s
    return {
        "x": nrm(ks[0], (BATCH, SEQ, D_MODEL), 1.0),
        "mix_norm": 1.0 + nrm(ks[1], (DEPTH, D_MODEL), 0.05),
        "ffn_norm": 1.0 + nrm(ks[2], (DEPTH, D_MODEL), 0.05),
        "even_w_in": nrm(ks[3], (N_EVEN, D_MODEL, EVEN_IN), D_MODEL ** -0.5),
        "even_conv_w": nrm(ks[4], (N_EVEN, CONV_WIDTH, CONV_CH), CONV_WIDTH ** -0.5),
        "even_conv_b": nrm(ks[5], (N_EVEN, CONV_CH), 0.01),
        "even_ln_g": 1.0 + nrm(ks[6], (N_EVEN, CONV_CH), 0.05),
        "even_ln_b": nrm(ks[7], (N_EVEN, CONV_CH), 0.01),
        "even_w_out": nrm(ks[8], (N_EVEN, EVEN_MIX, D_MODEL), EVEN_MIX ** -0.5),
        "odd_w_qkv": nrm(ks[9], (N_ODD, D_MODEL, 3 * SB_DIM), D_MODEL ** -0.5),
        "odd_w_o": nrm(ks[10], (N_ODD, SB_DIM, D_MODEL), SB_DIM ** -0.5),
        "ffn_w_up": nrm(ks[11], (DEPTH, D_MODEL, D_FF), D_MODEL ** -0.5),
        "ffn_w_gate": nrm(ks[12], (DEPTH, D_MODEL, D_FF), D_MODEL ** -0.5),
        "ffn_conv_w": nrm(ks[13], (DEPTH, FFN_CONV_WIDTH, D_FF), FFN_CONV_WIDTH ** -0.5),
        "ffn_conv_b": nrm(ks[14], (DEPTH, D_FF), 0.01),
        "ffn_w_down": nrm(ks[15], (DEPTH, D_FF, D_MODEL), D_FF ** -0.5),
        "final_norm": 1.0 + nrm(ks[16], (D_MODEL,), 0.05),
    }


def reference(x, mix_norm, ffn_norm, even_w_in, even_conv_w, even_conv_b, even_ln_g, even_ln_b, even_w_out,
              odd_w_qkv, odd_w_o, ffn_w_up, ffn_w_gate, ffn_conv_w, ffn_conv_b, ffn_w_down, final_norm):
    for layer in range(DEPTH):
        h = rms_norm(x, mix_norm[layer])
        j = layer // 2
        if layer % 2 == 0:
            x = x + even_mixer(h, even_w_in[j], even_conv_w[j], even_conv_b[j], even_ln_g[j], even_ln_b[j],
                               even_w_out[j])
        else:
            x = x + odd_mixer(h, odd_w_qkv[j], odd_w_o[j])
        h = rms_norm(x, ffn_norm[layer])
        x = x + conv_glu_ffn(h, ffn_w_up[layer], ffn_w_gate[layer], ffn_conv_w[layer], ffn_conv_b[layer],
                             ffn_w_down[layer])
    return rms_norm(x, final_norm)
```

```python
import functools

import jax
import jax.numpy as jnp
from jax import lax
from jax.experimental import pallas as pl
from jax.experimental.pallas import tpu as pltpu

F32 = jnp.float32
BF16 = jnp.bfloat16

HEAD_DIM = 128
CONV_CH = 1024
CONV_WIDTH = 31
MOBA_HEADS = 8
MOBA_BLOCK = 256
MOBA_TOPK = 3
SB_HEADS = 16
SB_TILE = 256
RMS_EPS = 1e-6
LN_EPS = 1e-5
LANES = 128
SUBLANES = 8
MASKED = -1e30
VMEM_LIMIT = 52 * 1024 * 1024

_NT = (((1,), (1,)), ((), ()))


def _params(*semantics):
    return pltpu.CompilerParams(dimension_semantics=semantics, vmem_limit_bytes=VMEM_LIMIT)


def _rms(x, g):
    ms = jnp.mean(x * x, axis=-1, keepdims=True)
    return x * lax.rsqrt(ms + RMS_EPS) * g


def _norm_matmul_kernel(x_ref, g_ref, w_ref, o_ref, h_ref):
    @pl.when(pl.program_id(1) == 0)
    def _():
        h_ref[...] = _rms(x_ref[...], g_ref[...]).astype(BF16)

    o_ref[...] = jnp.dot(h_ref[...], w_ref[...], preferred_element_type=F32).astype(o_ref.dtype)


def _norm_matmul(x, g, w, out_dtype, tm, tn):
    n, d = x.shape
    m = w.shape[1]
    return pl.pallas_call(
        _norm_matmul_kernel,
        grid=(n // tm, m // tn),
        in_specs=[pl.BlockSpec((tm, d), lambda i, j: (i, 0)),
                  pl.BlockSpec((1, d), lambda i, j: (0, 0)),
                  pl.BlockSpec((d, tn), lambda i, j: (0, j))],
        out_specs=pl.BlockSpec((tm, tn), lambda i, j: (i, j)),
        out_shape=jax.ShapeDtypeStruct((n, m), out_dtype),
        scratch_shapes=[pltpu.VMEM((tm, d), BF16)],
        compiler_params=_params("parallel", "arbitrary"),
    )(x, g, w)


def _matmul_res_kernel(*refs, n_pairs):
    a_refs, w_refs = refs[:n_pairs], refs[n_pairs:2 * n_pairs]
    x_ref, o_ref = refs[2 * n_pairs], refs[2 * n_pairs + 1]
    acc = x_ref[...]
    for a_ref, w_ref in zip(a_refs, w_refs):
        acc = acc + jnp.dot(a_ref[...], w_ref[...], preferred_element_type=F32)
    o_ref[...] = acc


def _matmul_res(a_list, w_list, x, tm, tn):
    n, m = x.shape
    n_pairs = len(a_list)
    in_specs = ([pl.BlockSpec((tm, a.shape[1]), lambda i, j: (i, 0)) for a in a_list]
                + [pl.BlockSpec((w.shape[0], tn), lambda i, j: (0, j)) for w in w_list]
                + [pl.BlockSpec((tm, tn), lambda i, j: (i, j))])
    return pl.pallas_call(
        functools.partial(_matmul_res_kernel, n_pairs=n_pairs),
        grid=(n // tm, m // tn),
        in_specs=in_specs,
        out_specs=pl.BlockSpec((tm, tn), lambda i, j: (i, j)),
        out_shape=jax.ShapeDtypeStruct((n, m), F32),
        compiler_params=_params("parallel", "arbitrary"),
    )(*a_list, *w_list, x)


def _conformer_kernel(a_ref, g_ref, w_ref, b_ref, lng_ref, lnb_ref, o_ref, buf_ref, hs_ref, *, tt, halo):
    n_chunks = buf_ref.shape[0]
    width = w_ref.shape[1]

    @pl.when(pl.program_id(1) == 0)
    def _():
        buf_ref[:, 0:halo, :] = jnp.zeros((n_chunks, halo, LANES), F32)

    glu = a_ref[0] * jax.nn.sigmoid(g_ref[0])
    for c in range(n_chunks):
        buf_ref[c, halo:halo + tt, :] = glu[:, c * LANES:(c + 1) * LANES]

    def chunk_body(c, carry):
        acc = jnp.broadcast_to(b_ref[c], (tt, LANES))
        for k in range(width):
            acc = acc + buf_ref[c, pl.ds(halo - (width - 1) + k, tt), :] * w_ref[c, pl.ds(k, 1), :]
        hs_ref[c] = acc
        buf_ref[c, 0:halo, :] = buf_ref[c, tt:tt + halo, :]
        return carry

    lax.fori_loop(0, n_chunks, chunk_body, 0)

    channels = n_chunks * LANES
    total = hs_ref[0]
    for c in range(1, n_chunks):
        total = total + hs_ref[c]
    mu = jnp.sum(total, axis=-1, keepdims=True) * (1.0 / channels)
    sq = jnp.zeros((tt, LANES), F32)
    for c in range(n_chunks):
        d = hs_ref[c] - mu
        sq = sq + d * d
    var = jnp.sum(sq, axis=-1, keepdims=True) * (1.0 / channels)
    inv = lax.rsqrt(var + LN_EPS)
    for c in range(n_chunks):
        y = (hs_ref[c] - mu) * inv * lng_ref[c] + lnb_ref[c]
        o_ref[0, :, c * LANES:(c + 1) * LANES] = (y * jax.nn.sigmoid(y)).astype(o_ref.dtype)


def _conformer(ag, conv_w, conv_b, ln_g, ln_b, tt):
    bsz, t_len, c2 = ag.shape
    ch = c2 // 2
    n_chunks = ch // LANES
    width = conv_w.shape[0]
    halo = -(-(width - 1) // SUBLANES) * SUBLANES
    by_chunk = lambda p: p.reshape(-1, n_chunks, LANES).transpose(1, 0, 2)
    small = lambda rows: pl.BlockSpec((n_chunks, rows, LANES), lambda b, t: (0, 0, 0))
    return pl.pallas_call(
        functools.partial(_conformer_kernel, tt=tt, halo=halo),
        grid=(bsz, t_len // tt),
        in_specs=[pl.BlockSpec((1, tt, ch), lambda b, t: (b, t, 0)),
                  pl.BlockSpec((1, tt, ch), lambda b, t: (b, t, 1)),
                  small(width), small(1), small(1), small(1)],
        out_specs=pl.BlockSpec((1, tt, ch), lambda b, t: (b, t, 0)),
        out_shape=jax.ShapeDtypeStruct((bsz, t_len, ch), BF16),
        scratch_shapes=[pltpu.VMEM((n_chunks, halo + tt, LANES), F32),
                        pltpu.VMEM((n_chunks, tt, LANES), F32)],
        compiler_params=_params("parallel", "arbitrary"),
    )(ag, ag, by_chunk(conv_w), by_chunk(conv_b), by_chunk(ln_g), by_chunk(ln_b))


def _moba_kernel(q_ref, k_ref, v_ref, o_ref, kaug_ref, kmean_ref, *, blk, topk):
    qi = pl.program_id(2)
    t_len = k_ref.shape[1]
    nb = t_len // blk
    scale = HEAD_DIM ** -0.5

    @pl.when(qi == 0)
    def _():
        k = k_ref[0]
        kaug_ref[:, 0:HEAD_DIM] = k
        row = lax.broadcasted_iota(jnp.int32, (t_len, LANES), 0)
        col = lax.broadcasted_iota(jnp.int32, (t_len, LANES), 1)
        kaug_ref[:, HEAD_DIM:HEAD_DIM + LANES] = jnp.where(row // blk == col, 1.0, 0.0).astype(BF16)
        kmean = jnp.mean(k.astype(F32).reshape(nb, blk, HEAD_DIM), axis=1)
        kmean_ref[...] = jnp.zeros(kmean_ref.shape, BF16)
        kmean_ref[0:nb, :] = kmean.astype(BF16)

    q = q_ref[0]
    gate = lax.dot_general(q, kmean_ref[...], _NT, preferred_element_type=F32)
    col = lax.broadcasted_iota(jnp.int32, gate.shape, 1)
    col_f = col.astype(F32)
    past = col < qi
    g = jnp.where(past, gate, -jnp.inf)
    allowed = col == qi
    for _ in range(topk):
        best = jnp.max(g, axis=1, keepdims=True)
        first = jnp.min(jnp.where(g == best, col_f, float(LANES)), axis=1, keepdims=True)
        pick = col_f == first
        allowed = jnp.logical_or(allowed, jnp.logical_and(pick, past))
        g = jnp.where(pick, -jnp.inf, g)
    penalty = jnp.where(allowed, 0.0, MASKED).astype(BF16)
    q_aug = jnp.concatenate([q, penalty], axis=1)

    def scores(j):
        kj = kaug_ref[pl.ds(pl.multiple_of(j * blk, blk), blk), :]
        return lax.dot_general(q_aug, kj, _NT, preferred_element_type=F32) * scale

    def values(j):
        return v_ref[0, pl.ds(pl.multiple_of(j * blk, blk), blk), :]

    r_i = lax.broadcasted_iota(jnp.int32, (blk, blk), 0)
    c_i = lax.broadcasted_iota(jnp.int32, (blk, blk), 1)
    s = jnp.where(c_i <= r_i, scores(qi), MASKED)
    m = jnp.max(s, axis=1, keepdims=True)
    p = jnp.exp(s - m)
    l = jnp.sum(p, axis=1, keepdims=True)
    acc = jnp.dot(p.astype(BF16), values(qi), preferred_element_type=F32)

    def body(j, carry):
        m, l, acc = carry
        s = scores(j)
        m_new = jnp.maximum(m, jnp.max(s, axis=1, keepdims=True))
        alpha = jnp.exp(m - m_new)
        p = jnp.exp(s - m_new)
        l = alpha * l + jnp.sum(p, axis=1, keepdims=True)
        acc = alpha * acc + jnp.dot(p.astype(BF16), values(j), preferred_element_type=F32)
        return m_new, l, acc

    m, l, acc = lax.fori_loop(0, qi, body, (m, l, acc))
    o_ref[0] = (acc / l).astype(o_ref.dtype)


def _moba(qkv, n_heads, blk, topk):
    bsz, t_len, _ = qkv.shape
    return pl.pallas_call(
        functools.partial(_moba_kernel, blk=blk, topk=topk),
        grid=(bsz, n_heads, t_len // blk),
        in_specs=[pl.BlockSpec((1, blk, HEAD_DIM), lambda b, h, i: (b, i, h)),
                  pl.BlockSpec((1, t_len, HEAD_DIM), lambda b, h, i: (b, 0, n_heads + h)),
                  pl.BlockSpec((1, t_len, HEAD_DIM), lambda b, h, i: (b, 0, 2 * n_heads + h))],
        out_specs=pl.BlockSpec((1, blk, HEAD_DIM), lambda b, h, i: (b, i, h)),
        out_shape=jax.ShapeDtypeStruct((bsz, t_len, n_heads * HEAD_DIM), BF16),
        scratch_shapes=[pltpu.VMEM((t_len, HEAD_DIM + LANES), BF16),
                        pltpu.VMEM((LANES, HEAD_DIM), BF16)],
        compiler_params=_params("parallel", "parallel", "arbitrary"),
    )(qkv, qkv, qkv)


def _sb_kernel(q_ref, k_ref, v_ref, o_ref, *, tile):
    qi = pl.program_id(2)
    scale = HEAD_DIM ** -0.5
    q = q_ref[0]
    r_i = lax.broadcasted_iota(jnp.int32, (tile, tile), 0)
    c_i = lax.broadcasted_iota(jnp.int32, (tile, tile), 1)
    later = jnp.where(r_i > c_i, 1.0, 0.0).astype(BF16)
    strict = c_i < r_i

    def tile_update(j, run, acc, diagonal):
        ks = pl.ds(pl.multiple_of(j * tile, tile), tile)
        z = lax.dot_general(q, k_ref[0, ks, :], _NT, preferred_element_type=F32) * scale
        softplus = jnp.maximum(z, 0.0) + jnp.log(1.0 + jnp.exp(-jnp.abs(z)))
        log_1m = -softplus
        if diagonal:
            log_1m = jnp.where(strict, log_1m, 0.0)
        hi = log_1m.astype(BF16)
        lo = (log_1m - hi.astype(F32)).astype(BF16)
        after = (jnp.dot(hi, later, preferred_element_type=F32)
                 + jnp.dot(lo, later, preferred_element_type=F32))
        a = jnp.exp(z - softplus + after + run)
        if diagonal:
            a = jnp.where(strict, a, 0.0)
        acc = acc + jnp.dot(a.astype(BF16), v_ref[0, ks, :], preferred_element_type=F32)
        run = run + jnp.sum(log_1m, axis=1, keepdims=True)
        return run, acc

    run, acc = tile_update(qi, jnp.zeros((tile, 1), F32), jnp.zeros((tile, HEAD_DIM), F32), True)

    def body(step, carry):
        return tile_update(qi - 1 - step, *carry, False)

    run, acc = lax.fori_loop(0, qi, body, (run, acc))
    o_ref[0] = acc.astype(o_ref.dtype)


def _stick_breaking(qkv, n_heads, tile):
    bsz, t_len, _ = qkv.shape
    return pl.pallas_call(
        functools.partial(_sb_kernel, tile=tile),
        grid=(bsz, n_heads, t_len // tile),
        in_specs=[pl.BlockSpec((1, tile, HEAD_DIM), lambda b, h, i: (b, i, h)),
                  pl.BlockSpec((1, t_len, HEAD_DIM), lambda b, h, i: (b, 0, n_heads + h)),
                  pl.BlockSpec((1, t_len, HEAD_DIM), lambda b, h, i: (b, 0, 2 * n_heads + h))],
        out_specs=pl.BlockSpec((1, tile, HEAD_DIM), lambda b, h, i: (b, i, h)),
        out_shape=jax.ShapeDtypeStruct((bsz, t_len, n_heads * HEAD_DIM), BF16),
        compiler_params=_params("parallel", "parallel", "arbitrary"),
    )(qkv, qkv, qkv)


def _ffn_kernel(*refs, tm, tiles_per_seq, final):
    if final:
        x_ref, g_ref, wu_ref, wg_ref, cw_ref, cb_ref, wd_ref, fg_ref, o_ref, h_ref, ubuf_ref, halo_ref = refs
    else:
        x_ref, g_ref, wu_ref, wg_ref, cw_ref, cb_ref, wd_ref, o_ref, h_ref, ubuf_ref, halo_ref = refs
    i = pl.program_id(0)
    f = pl.program_id(1)
    width = cw_ref.shape[0]

    @pl.when(f == 0)
    def _():
        x = x_ref[...]
        h_ref[...] = _rms(x, g_ref[...]).astype(BF16)
        o_ref[...] = x

    h = h_ref[...]
    up = jnp.dot(h, wu_ref[...], preferred_element_type=F32)
    gate = jnp.dot(h, wg_ref[...], preferred_element_type=F32)

    seq_start = (i % tiles_per_seq) == 0

    @pl.when(seq_start)
    def _():
        ubuf_ref[0:SUBLANES, :] = jnp.zeros((SUBLANES, ubuf_ref.shape[1]), F32)

    @pl.when(jnp.logical_not(seq_start))
    def _():
        ubuf_ref[0:SUBLANES, :] = halo_ref[f]

    ubuf_ref[SUBLANES:SUBLANES + tm, :] = up
    halo_ref[f] = up[tm - SUBLANES:tm, :]

    conv = up * cw_ref[width - 1:width, :] + cb_ref[...]
    for k in range(width - 1):
        conv = conv + ubuf_ref[pl.ds(SUBLANES - (width - 1) + k, tm), :] * cw_ref[k:k + 1, :]
    act = (conv * jax.nn.sigmoid(conv) * gate).astype(BF16)
    o_ref[...] += jnp.dot(act, wd_ref[...], preferred_element_type=F32)

    if final:
        @pl.when(f == pl.num_programs(1) - 1)
        def _():
            o_ref[...] = _rms(o_ref[...], fg_ref[...])


def _ffn(x, g, w_up, w_gate, conv_w, conv_b, w_down, final_g, t_len, tm, tf):
    n, d = x.shape
    d_ff = w_up.shape[1]
    width = conv_w.shape[0]
    final = final_g is not None
    row = lambda i, f: (i, 0)
    in_specs = [pl.BlockSpec((tm, d), row),
                pl.BlockSpec((1, d), lambda i, f: (0, 0)),
                pl.BlockSpec((d, tf), lambda i, f: (0, f)),
                pl.BlockSpec((d, tf), lambda i, f: (0, f)),
                pl.BlockSpec((width, tf), lambda i, f: (0, f)),
                pl.BlockSpec((1, tf), lambda i, f: (0, f)),
                pl.BlockSpec((tf, d), lambda i, f: (f, 0))]
    args = [x, g, w_up, w_gate, conv_w, conv_b, w_down]
    if final:
        in_specs.append(pl.BlockSpec((1, d), lambda i, f: (0, 0)))
        args.append(final_g)
    return pl.pallas_call(
        functools.partial(_ffn_kernel, tm=tm, tiles_per_seq=t_len // tm, final=final),
        grid=(n // tm, d_ff // tf),
        in_specs=in_specs,
        out_specs=pl.BlockSpec((tm, d), row),
        out_shape=jax.ShapeDtypeStruct((n, d), F32),
        scratch_shapes=[pltpu.VMEM((tm, d), BF16),
                        pltpu.VMEM((SUBLANES + tm, tf), F32),
                        pltpu.VMEM((d_ff // tf, SUBLANES, tf), F32)],
        compiler_params=_params("arbitrary", "arbitrary"),
    )(*args)


def kernel(x, mix_norm, ffn_norm, even_w_in, even_conv_w, even_conv_b, even_ln_g, even_ln_b, even_w_out,
           odd_w_qkv, odd_w_o, ffn_w_up, ffn_w_gate, ffn_conv_w, ffn_conv_b, ffn_w_down, final_norm):
    bsz, t_len, d = x.shape
    n = bsz * t_len
    depth = mix_norm.shape[0]
    xs = x.reshape(n, d)
    for layer in range(depth):
        j = layer // 2
        g = mix_norm[layer][None, :]
        if layer % 2 == 0:
            w_in = even_w_in[j]
            ag = _norm_matmul(xs, g, w_in[:, :2 * CONV_CH].astype(BF16), F32, 1024, 1024)
            qkv = _norm_matmul(xs, g, w_in[:, 2 * CONV_CH:].astype(BF16), BF16, 1024, 1024)
            y_a = _conformer(ag.reshape(bsz, t_len, -1), even_conv_w[j], even_conv_b[j][None, :],
                             even_ln_g[j][None, :], even_ln_b[j][None, :], 256)
            y_b = _moba(qkv.reshape(bsz, t_len, -1), MOBA_HEADS, MOBA_BLOCK, MOBA_TOPK)
            w_out = even_w_out[j].astype(BF16)
            xs = _matmul_res([y_a.reshape(n, -1), y_b.reshape(n, -1)], [w_out[:CONV_CH], w_out[CONV_CH:]],
                             xs, 1024, 1024)
        else:
            qkv = _norm_matmul(xs, g, odd_w_qkv[j].astype(BF16), BF16, 1024, 1024)
            y = _stick_breaking(qkv.reshape(bsz, t_len, -1), SB_HEADS, SB_TILE)
            xs = _matmul_res([y.reshape(n, -1)], [odd_w_o[j].astype(BF16)], xs, 1024, 1024)
        final_g = final_norm[None, :] if layer == depth - 1 else None
        xs = _ffn(xs, ffn_norm[layer][None, :], ffn_w_up[layer].astype(BF16), ffn_w_gate[layer].astype(BF16),
                  ffn_conv_w[layer], ffn_conv_b[layer][None, :], ffn_w_down[layer].astype(BF16),
                  final_g, t_len, 512, 512)
    return xs.reshape(bsz, t_len, d)
```

```python
import functools

import jax
import jax.numpy as jnp
from jax import lax
from jax.experimental import pallas as pl
from jax.experimental.pallas import tpu as pltpu

F32 = jnp.float32
BF16 = jnp.bfloat16

HEAD_DIM = 128
CONV_CH = 1024
CONV_WIDTH = 31
MOBA_HEADS = 8
MOBA_BLOCK = 256
MOBA_TOPK = 3
SB_HEADS = 16
SB_TILE = 256
RMS_EPS = 1e-6
LN_EPS = 1e-5
LANES = 128
SUBLANES = 8
MASKED = -1e30
SB_EXP_IS_ZERO = -105.0
VMEM_LIMIT = 52 * 1024 * 1024

_NT = (((1,), (1,)), ((), ()))


def _params(*semantics):
    return pltpu.CompilerParams(dimension_semantics=semantics, vmem_limit_bytes=VMEM_LIMIT)


def _rms(x, g):
    ms = jnp.mean(x * x, axis=-1, keepdims=True)
    return x * lax.rsqrt(ms + RMS_EPS) * g


def _norm_matmul_kernel(x_ref, g_ref, w_ref, o_ref, h_ref):
    @pl.when(pl.program_id(1) == 0)
    def _():
        h_ref[...] = _rms(x_ref[...], g_ref[...]).astype(BF16)

    o_ref[...] = jnp.dot(h_ref[...], w_ref[...], preferred_element_type=F32).astype(o_ref.dtype)


def _norm_matmul(x, g, w, out_dtype, tm, tn):
    n, d = x.shape
    m = w.shape[1]
    return pl.pallas_call(
        _norm_matmul_kernel,
        grid=(n // tm, m // tn),
        in_specs=[pl.BlockSpec((tm, d), lambda i, j: (i, 0)),
                  pl.BlockSpec((1, d), lambda i, j: (0, 0)),
                  pl.BlockSpec((d, tn), lambda i, j: (0, j))],
        out_specs=pl.BlockSpec((tm, tn), lambda i, j: (i, j)),
        out_shape=jax.ShapeDtypeStruct((n, m), out_dtype),
        scratch_shapes=[pltpu.VMEM((tm, d), BF16)],
        compiler_params=_params("parallel", "arbitrary"),
    )(x, g, w)


def _matmul_res_kernel(*refs, n_pairs):
    a_refs, w_refs = refs[:n_pairs], refs[n_pairs:2 * n_pairs]
    x_ref, o_ref = refs[2 * n_pairs], refs[2 * n_pairs + 1]
    acc = x_ref[...]
    for a_ref, w_ref in zip(a_refs, w_refs):
        acc = acc + jnp.dot(a_ref[...], w_ref[...], preferred_element_type=F32)
    o_ref[...] = acc


def _matmul_res(a_list, w_list, x, tm, tn):
    n, m = x.shape
    n_pairs = len(a_list)
    in_specs = ([pl.BlockSpec((tm, a.shape[1]), lambda i, j: (i, 0)) for a in a_list]
                + [pl.BlockSpec((w.shape[0], tn), lambda i, j: (0, j)) for w in w_list]
                + [pl.BlockSpec((tm, tn), lambda i, j: (i, j))])
    return pl.pallas_call(
        functools.partial(_matmul_res_kernel, n_pairs=n_pairs),
        grid=(n // tm, m // tn),
        in_specs=in_specs,
        out_specs=pl.BlockSpec((tm, tn), lambda i, j: (i, j)),
        out_shape=jax.ShapeDtypeStruct((n, m), F32),
        compiler_params=_params("parallel", "arbitrary"),
    )(*a_list, *w_list, x)


def _conformer_kernel(a_ref, g_ref, w_ref, b_ref, lng_ref, lnb_ref, o_ref, buf_ref, hs_ref, *, tt, halo):
    n_chunks = buf_ref.shape[0]
    width = w_ref.shape[1]

    @pl.when(pl.program_id(1) == 0)
    def _():
        buf_ref[:, 0:halo, :] = jnp.zeros((n_chunks, halo, LANES), F32)

    glu = a_ref[0] * jax.nn.sigmoid(g_ref[0])
    for c in range(n_chunks):
        buf_ref[c, halo:halo + tt, :] = glu[:, c * LANES:(c + 1) * LANES]

    def chunk_body(c, carry):
        acc = jnp.broadcast_to(b_ref[c], (tt, LANES))
        for k in range(width):
            acc = acc + buf_ref[c, pl.ds(halo - (width - 1) + k, tt), :] * w_ref[c, pl.ds(k, 1), :]
        hs_ref[c] = acc
        buf_ref[c, 0:halo, :] = buf_ref[c, tt:tt + halo, :]
        return carry

    lax.fori_loop(0, n_chunks, chunk_body, 0)

    channels = n_chunks * LANES
    total = hs_ref[0]
    for c in range(1, n_chunks):
        total = total + hs_ref[c]
    mu = jnp.sum(total, axis=-1, keepdims=True) * (1.0 / channels)
    sq = jnp.zeros((tt, LANES), F32)
    for c in range(n_chunks):
        d = hs_ref[c] - mu
        sq = sq + d * d
    var = jnp.sum(sq, axis=-1, keepdims=True) * (1.0 / channels)
    inv = lax.rsqrt(var + LN_EPS)
    for c in range(n_chunks):
        y = (hs_ref[c] - mu) * inv * lng_ref[c] + lnb_ref[c]
        o_ref[0, :, c * LANES:(c + 1) * LANES] = (y * jax.nn.sigmoid(y)).astype(o_ref.dtype)


def _conformer(ag, conv_w, conv_b, ln_g, ln_b, tt):
    bsz, t_len, c2 = ag.shape
    ch = c2 // 2
    n_chunks = ch // LANES
    width = conv_w.shape[0]
    halo = -(-(width - 1) // SUBLANES) * SUBLANES
    by_chunk = lambda p: p.reshape(-1, n_chunks, LANES).transpose(1, 0, 2)
    small = lambda rows: pl.BlockSpec((n_chunks, rows, LANES), lambda b, t: (0, 0, 0))
    return pl.pallas_call(
        functools.partial(_conformer_kernel, tt=tt, halo=halo),
        grid=(bsz, t_len // tt),
        in_specs=[pl.BlockSpec((1, tt, ch), lambda b, t: (b, t, 0)),
                  pl.BlockSpec((1, tt, ch), lambda b, t: (b, t, 1)),
                  small(width), small(1), small(1), small(1)],
        out_specs=pl.BlockSpec((1, tt, ch), lambda b, t: (b, t, 0)),
        out_shape=jax.ShapeDtypeStruct((bsz, t_len, ch), BF16),
        scratch_shapes=[pltpu.VMEM((n_chunks, halo + tt, LANES), F32),
                        pltpu.VMEM((n_chunks, tt, LANES), F32)],
        compiler_params=_params("parallel", "arbitrary"),
    )(ag, ag, by_chunk(conv_w), by_chunk(conv_b), by_chunk(ln_g), by_chunk(ln_b))


def _moba_kernel(q_ref, k_ref, v_ref, o_ref, kaug_ref, kmean_ref, *, blk, topk):
    qi = pl.program_id(2)
    t_len = k_ref.shape[1]
    nb = t_len // blk
    scale = HEAD_DIM ** -0.5

    @pl.when(qi == 0)
    def _():
        k = k_ref[0]
        kaug_ref[:, 0:HEAD_DIM] = k
        row = lax.broadcasted_iota(jnp.int32, (t_len, LANES), 0)
        col = lax.broadcasted_iota(jnp.int32, (t_len, LANES), 1)
        kaug_ref[:, HEAD_DIM:HEAD_DIM + LANES] = jnp.where(row // blk == col, 1.0, 0.0).astype(BF16)
        kmean = jnp.mean(k.astype(F32).reshape(nb, blk, HEAD_DIM), axis=1)
        kmean_ref[...] = jnp.zeros(kmean_ref.shape, BF16)
        kmean_ref[0:nb, :] = kmean.astype(BF16)

    q = q_ref[0]
    gate = lax.dot_general(q, kmean_ref[...], _NT, preferred_element_type=F32)
    col = lax.broadcasted_iota(jnp.int32, gate.shape, 1)
    col_f = col.astype(F32)
    past = col < qi
    g = jnp.where(past, gate, -jnp.inf)
    allowed = col == qi
    for _ in range(topk):
        best = jnp.max(g, axis=1, keepdims=True)
        first = jnp.min(jnp.where(g == best, col_f, float(LANES)), axis=1, keepdims=True)
        pick = col_f == first
        allowed = jnp.logical_or(allowed, jnp.logical_and(pick, past))
        g = jnp.where(pick, -jnp.inf, g)
    penalty = jnp.where(allowed, 0.0, MASKED).astype(BF16)
    q_aug = jnp.concatenate([q, penalty], axis=1)

    def scores(j):
        kj = kaug_ref[pl.ds(pl.multiple_of(j * blk, blk), blk), :]
        return lax.dot_general(q_aug, kj, _NT, preferred_element_type=F32) * scale

    def values(j):
        return v_ref[0, pl.ds(pl.multiple_of(j * blk, blk), blk), :]

    r_i = lax.broadcasted_iota(jnp.int32, (blk, blk), 0)
    c_i = lax.broadcasted_iota(jnp.int32, (blk, blk), 1)
    s = jnp.where(c_i <= r_i, scores(qi), MASKED)
    m = jnp.max(s, axis=1, keepdims=True)
    p = jnp.exp(s - m)
    l = jnp.sum(p, axis=1, keepdims=True)
    acc = jnp.dot(p.astype(BF16), values(qi), preferred_element_type=F32)

    def body(j, carry):
        m, l, acc = carry
        s = scores(j)
        m_new = jnp.maximum(m, jnp.max(s, axis=1, keepdims=True))
        alpha = jnp.exp(m - m_new)
        p = jnp.exp(s - m_new)
        l = alpha * l + jnp.sum(p, axis=1, keepdims=True)
        acc = alpha * acc + jnp.dot(p.astype(BF16), values(j), preferred_element_type=F32)
        return m_new, l, acc

    m, l, acc = lax.fori_loop(0, qi, body, (m, l, acc))
    o_ref[0] = (acc / l).astype(o_ref.dtype)


def _moba(qkv, n_heads, blk, topk):
    bsz, t_len, _ = qkv.shape
    return pl.pallas_call(
        functools.partial(_moba_kernel, blk=blk, topk=topk),
        grid=(bsz, n_heads, t_len // blk),
        in_specs=[pl.BlockSpec((1, blk, HEAD_DIM), lambda b, h, i: (b, i, h)),
                  pl.BlockSpec((1, t_len, HEAD_DIM), lambda b, h, i: (b, 0, n_heads + h)),
                  pl.BlockSpec((1, t_len, HEAD_DIM), lambda b, h, i: (b, 0, 2 * n_heads + h))],
        out_specs=pl.BlockSpec((1, blk, HEAD_DIM), lambda b, h, i: (b, i, h)),
        out_shape=jax.ShapeDtypeStruct((bsz, t_len, n_heads * HEAD_DIM), BF16),
        scratch_shapes=[pltpu.VMEM((t_len, HEAD_DIM + LANES), BF16),
                        pltpu.VMEM((LANES, HEAD_DIM), BF16)],
        compiler_params=_params("parallel", "parallel", "arbitrary"),
    )(qkv, qkv, qkv)


def _sb_kernel(q_ref, k_ref, v_ref, o_ref, *, tile):
    qi = pl.program_id(2)
    scale = HEAD_DIM ** -0.5
    q = q_ref[0]
    r_i = lax.broadcasted_iota(jnp.int32, (tile, tile), 0)
    c_i = lax.broadcasted_iota(jnp.int32, (tile, tile), 1)
    later = jnp.where(r_i > c_i, 1.0, 0.0).astype(BF16)
    strict = c_i < r_i

    def tile_update(j, run, acc, diagonal):
        ks = pl.ds(pl.multiple_of(j * tile, tile), tile)
        z = lax.dot_general(q, k_ref[0, ks, :], _NT, preferred_element_type=F32) * scale
        softplus = jnp.maximum(z, 0.0) + jnp.log(1.0 + jnp.exp(-jnp.abs(z)))
        log_1m = -softplus
        if diagonal:
            log_1m = jnp.where(strict, log_1m, 0.0)
        hi = log_1m.astype(BF16)
        lo = (log_1m - hi.astype(F32)).astype(BF16)
        after = (jnp.dot(hi, later, preferred_element_type=F32)
                 + jnp.dot(lo, later, preferred_element_type=F32))
        a = jnp.exp(z - softplus + after + run)
        if diagonal:
            a = jnp.where(strict, a, 0.0)
        acc = acc + jnp.dot(a.astype(BF16), v_ref[0, ks, :], preferred_element_type=F32)
        run = run + jnp.sum(log_1m, axis=1, keepdims=True)
        return run, acc

    run, acc = tile_update(qi, jnp.zeros((tile, 1), F32), jnp.zeros((tile, HEAD_DIM), F32), True)

    def live(run):
        return jnp.max(run) > SB_EXP_IS_ZERO

    def cond(carry):
        j, go, _, _ = carry
        return jnp.logical_and(j >= 0, go)

    def body(carry):
        j, _, run, acc = carry
        run, acc = tile_update(j, run, acc, False)
        return j - 1, live(run), run, acc

    _, _, run, acc = lax.while_loop(cond, body, (qi - 1, live(run), run, acc))
    o_ref[0] = acc.astype(o_ref.dtype)


def _stick_breaking(qkv, n_heads, tile):
    bsz, t_len, _ = qkv.shape
    return pl.pallas_call(
        functools.partial(_sb_kernel, tile=tile),
        grid=(bsz, n_heads, t_len // tile),
        in_specs=[pl.BlockSpec((1, tile, HEAD_DIM), lambda b, h, i: (b, i, h)),
                  pl.BlockSpec((1, t_len, HEAD_DIM), lambda b, h, i: (b, 0, n_heads + h)),
                  pl.BlockSpec((1, t_len, HEAD_DIM), lambda b, h, i: (b, 0, 2 * n_heads + h))],
        out_specs=pl.BlockSpec((1, tile, HEAD_DIM), lambda b, h, i: (b, i, h)),
        out_shape=jax.ShapeDtypeStruct((bsz, t_len, n_heads * HEAD_DIM), BF16),
        compiler_params=_params("parallel", "parallel", "arbitrary"),
    )(qkv, qkv, qkv)


def _ffn_kernel(*refs, tm, tiles_per_seq, final):
    if final:
        x_ref, g_ref, wu_ref, wg_ref, cw_ref, cb_ref, wd_ref, fg_ref, o_ref, h_ref, ubuf_ref, halo_ref = refs
    else:
        x_ref, g_ref, wu_ref, wg_ref, cw_ref, cb_ref, wd_ref, o_ref, h_ref, ubuf_ref, halo_ref = refs
    i = pl.program_id(0)
    f = pl.program_id(1)
    width = cw_ref.shape[0]

    @pl.when(f == 0)
    def _():
        x = x_ref[...]
        h_ref[...] = _rms(x, g_ref[...]).astype(BF16)
        o_ref[...] = x

    h = h_ref[...]
    up = jnp.dot(h, wu_ref[...], preferred_element_type=F32)
    gate = jnp.dot(h, wg_ref[...], preferred_element_type=F32)

    seq_start = (i % tiles_per_seq) == 0

    @pl.when(seq_start)
    def _():
        ubuf_ref[0:SUBLANES, :] = jnp.zeros((SUBLANES, ubuf_ref.shape[1]), F32)

    @pl.when(jnp.logical_not(seq_start))
    def _():
        ubuf_ref[0:SUBLANES, :] = halo_ref[f]

    ubuf_ref[SUBLANES:SUBLANES + tm, :] = up
    halo_ref[f] = up[tm - SUBLANES:tm, :]

    conv = up * cw_ref[width - 1:width, :] + cb_ref[...]
    for k in range(width - 1):
        conv = conv + ubuf_ref[pl.ds(SUBLANES - (width - 1) + k, tm), :] * cw_ref[k:k + 1, :]
    act = (conv * jax.nn.sigmoid(conv) * gate).astype(BF16)
    o_ref[...] += jnp.dot(act, wd_ref[...], preferred_element_type=F32)

    if final:
        @pl.when(f == pl.num_programs(1) - 1)
        def _():
            o_ref[...] = _rms(o_ref[...], fg_ref[...])


def _ffn(x, g, w_up, w_gate, conv_w, conv_b, w_down, final_g, t_len, tm, tf):
    n, d = x.shape
    d_ff = w_up.shape[1]
    width = conv_w.shape[0]
    final = final_g is not None
    row = lambda i, f: (i, 0)
    in_specs = [pl.BlockSpec((tm, d), row),
                pl.BlockSpec((1, d), lambda i, f: (0, 0)),
                pl.BlockSpec((d, tf), lambda i, f: (0, f)),
                pl.BlockSpec((d, tf), lambda i, f: (0, f)),
                pl.BlockSpec((width, tf), lambda i, f: (0, f)),
                pl.BlockSpec((1, tf), lambda i, f: (0, f)),
                pl.BlockSpec((tf, d), lambda i, f: (f, 0))]
    args = [x, g, w_up, w_gate, conv_w, conv_b, w_down]
    if final:
        in_specs.append(pl.BlockSpec((1, d), lambda i, f: (0, 0)))
        args.append(final_g)
    return pl.pallas_call(
        functools.partial(_ffn_kernel, tm=tm, tiles_per_seq=t_len // tm, final=final),
        grid=(n // tm, d_ff // tf),
        in_specs=in_specs,
        out_specs=pl.BlockSpec((tm, d), row),
        out_shape=jax.ShapeDtypeStruct((n, d), F32),
        scratch_shapes=[pltpu.VMEM((tm, d), BF16),
                        pltpu.VMEM((SUBLANES + tm, tf), F32),
                        pltpu.VMEM((d_ff // tf, SUBLANES, tf), F32)],
        compiler_params=_params("arbitrary", "arbitrary"),
    )(*args)


def kernel(x, mix_norm, ffn_norm, even_w_in, even_conv_w, even_conv_b, even_ln_g, even_ln_b, even_w_out,
           odd_w_qkv, odd_w_o, ffn_w_up, ffn_w_gate, ffn_conv_w, ffn_conv_b, ffn_w_down, final_norm):
    bsz, t_len, d = x.shape
    n = bsz * t_len
    depth = mix_norm.shape[0]
    xs = x.reshape(n, d)
    for layer in range(depth):
        j = layer // 2
        g = mix_norm[layer][None, :]
        if layer % 2 == 0:
            w_in = even_w_in[j]
            ag = _norm_matmul(xs, g, w_in[:, :2 * CONV_CH].astype(BF16), F32, 1024, 1024)
            qkv = _norm_matmul(xs, g, w_in[:, 2 * CONV_CH:].astype(BF16), BF16, 1024, 1024)
            y_a = _conformer(ag.reshape(bsz, t_len, -1), even_conv_w[j], even_conv_b[j][None, :],
                             even_ln_g[j][None, :], even_ln_b[j][None, :], 256)
            y_b = _moba(qkv.reshape(bsz, t_len, -1), MOBA_HEADS, MOBA_BLOCK, MOBA_TOPK)
            w_out = even_w_out[j].astype(BF16)
            xs = _matmul_res([y_a.reshape(n, -1), y_b.reshape(n, -1)], [w_out[:CONV_CH], w_out[CONV_CH:]],
                             xs, 1024, 1024)
        else:
            qkv = _norm_matmul(xs, g, odd_w_qkv[j].astype(BF16), BF16, 1024, 1024)
            y = _stick_breaking(qkv.reshape(bsz, t_len, -1), SB_HEADS, SB_TILE)
            xs = _matmul_res([y.reshape(n, -1)], [odd_w_o[j].astype(BF16)], xs, 1024, 1024)
        final_g = final_norm[None, :] if layer == depth - 1 else None
        xs = _ffn(xs, ffn_norm[layer][None, :], ffn_w_up[layer].astype(BF16), ffn_w_gate[layer].astype(BF16),
                  ffn_conv_w[layer], ffn_conv_b[layer][None, :], ffn_w_down[layer].astype(BF16),
                  final_g, t_len, 512, 512)
    return xs.reshape(bsz, t_len, d)
```

```python
import functools

import jax
import jax.numpy as jnp
from jax import lax
from jax.experimental import pallas as pl
from jax.experimental.pallas import tpu as pltpu

F32 = jnp.float32
BF16 = jnp.bfloat16

HEAD_DIM = 128
CONV_CH = 1024
CONV_WIDTH = 31
MOBA_HEADS = 8
MOBA_BLOCK = 256
MOBA_TOPK = 3
MOBA_KEY_TILE = 1024
MOBA_HEADS_PER_STEP = 4
SB_HEADS = 16
SB_TILE = 256
SB_HEADS_PER_STEP = 4
RMS_EPS = 1e-6
LN_EPS = 1e-5
LANES = 128
SUBLANES = 8
MASKED = -1e30
LOG2_E = 1.4426950408889634
SB_EXP_IS_ZERO = -105.0
VMEM_LIMIT = 52 * 1024 * 1024

_NT = (((1,), (1,)), ((), ()))


def _params(*semantics):
    return pltpu.CompilerParams(dimension_semantics=semantics, vmem_limit_bytes=VMEM_LIMIT)


def _rms(x, g):
    ms = jnp.mean(x * x, axis=-1, keepdims=True)
    return x * lax.rsqrt(ms + RMS_EPS) * g


def _norm_matmul_kernel(x_ref, g_ref, w_ref, o_ref, h_ref):
    @pl.when(pl.program_id(1) == 0)
    def _():
        h_ref[...] = _rms(x_ref[...], g_ref[...]).astype(BF16)

    o_ref[...] = jnp.dot(h_ref[...], w_ref[...], preferred_element_type=F32).astype(o_ref.dtype)


def _norm_matmul(x, g, w, out_dtype, tm, tn):
    n, d = x.shape
    m = w.shape[1]
    return pl.pallas_call(
        _norm_matmul_kernel,
        grid=(n // tm, m // tn),
        in_specs=[pl.BlockSpec((tm, d), lambda i, j: (i, 0)),
                  pl.BlockSpec((1, d), lambda i, j: (0, 0)),
                  pl.BlockSpec((d, tn), lambda i, j: (0, j))],
        out_specs=pl.BlockSpec((tm, tn), lambda i, j: (i, j)),
        out_shape=jax.ShapeDtypeStruct((n, m), out_dtype),
        scratch_shapes=[pltpu.VMEM((tm, d), BF16)],
        compiler_params=_params("parallel", "arbitrary"),
    )(x, g, w)


def _matmul_res_kernel(*refs, n_pairs):
    a_refs, w_refs = refs[:n_pairs], refs[n_pairs:2 * n_pairs]
    x_ref, o_ref = refs[2 * n_pairs], refs[2 * n_pairs + 1]
    acc = x_ref[...]
    for a_ref, w_ref in zip(a_refs, w_refs):
        acc = acc + jnp.dot(a_ref[...], w_ref[...], preferred_element_type=F32)
    o_ref[...] = acc


def _matmul_res(a_list, w_list, x, tm, tn):
    n, m = x.shape
    n_pairs = len(a_list)
    in_specs = ([pl.BlockSpec((tm, a.shape[1]), lambda i, j: (i, 0)) for a in a_list]
                + [pl.BlockSpec((w.shape[0], tn), lambda i, j: (0, j)) for w in w_list]
                + [pl.BlockSpec((tm, tn), lambda i, j: (i, j))])
    return pl.pallas_call(
        functools.partial(_matmul_res_kernel, n_pairs=n_pairs),
        grid=(n // tm, m // tn),
        in_specs=in_specs,
        out_specs=pl.BlockSpec((tm, tn), lambda i, j: (i, j)),
        out_shape=jax.ShapeDtypeStruct((n, m), F32),
        compiler_params=_params("parallel", "arbitrary"),
    )(*a_list, *w_list, x)


def _conformer_kernel(a_ref, g_ref, w_ref, b_ref, lng_ref, lnb_ref, o_ref, buf_ref, hs_ref, *, tt, halo):
    n_chunks = buf_ref.shape[0]
    width = w_ref.shape[1]

    @pl.when(pl.program_id(1) == 0)
    def _():
        buf_ref[:, 0:halo, :] = jnp.zeros((n_chunks, halo, LANES), F32)

    glu = a_ref[0] * jax.nn.sigmoid(g_ref[0])
    for c in range(n_chunks):
        buf_ref[c, halo:halo + tt, :] = glu[:, c * LANES:(c + 1) * LANES]

    def chunk_body(c, carry):
        acc = jnp.broadcast_to(b_ref[c], (tt, LANES))
        for k in range(width):
            acc = acc + buf_ref[c, pl.ds(halo - (width - 1) + k, tt), :] * w_ref[c, pl.ds(k, 1), :]
        hs_ref[c] = acc
        buf_ref[c, 0:halo, :] = buf_ref[c, tt:tt + halo, :]
        return carry

    lax.fori_loop(0, n_chunks, chunk_body, 0)

    channels = n_chunks * LANES
    total = hs_ref[0]
    for c in range(1, n_chunks):
        total = total + hs_ref[c]
    mu = jnp.sum(total, axis=-1, keepdims=True) * (1.0 / channels)
    sq = jnp.zeros((tt, LANES), F32)
    for c in range(n_chunks):
        d = hs_ref[c] - mu
        sq = sq + d * d
    var = jnp.sum(sq, axis=-1, keepdims=True) * (1.0 / channels)
    inv = lax.rsqrt(var + LN_EPS)
    for c in range(n_chunks):
        y = (hs_ref[c] - mu) * inv * lng_ref[c] + lnb_ref[c]
        o_ref[0, :, c * LANES:(c + 1) * LANES] = (y * jax.nn.sigmoid(y)).astype(o_ref.dtype)


def _conformer(ag, conv_w, conv_b, ln_g, ln_b, tt):
    bsz, t_len, c2 = ag.shape
    ch = c2 // 2
    n_chunks = ch // LANES
    width = conv_w.shape[0]
    halo = -(-(width - 1) // SUBLANES) * SUBLANES
    by_chunk = lambda p: p.reshape(-1, n_chunks, LANES).transpose(1, 0, 2)
    small = lambda rows: pl.BlockSpec((n_chunks, rows, LANES), lambda b, t: (0, 0, 0))
    return pl.pallas_call(
        functools.partial(_conformer_kernel, tt=tt, halo=halo),
        grid=(bsz, t_len // tt),
        in_specs=[pl.BlockSpec((1, tt, ch), lambda b, t: (b, t, 0)),
                  pl.BlockSpec((1, tt, ch), lambda b, t: (b, t, 1)),
                  small(width), small(1), small(1), small(1)],
        out_specs=pl.BlockSpec((1, tt, ch), lambda b, t: (b, t, 0)),
        out_shape=jax.ShapeDtypeStruct((bsz, t_len, ch), BF16),
        scratch_shapes=[pltpu.VMEM((n_chunks, halo + tt, LANES), F32),
                        pltpu.VMEM((n_chunks, tt, LANES), F32)],
        compiler_params=_params("parallel", "arbitrary"),
    )(ag, ag, by_chunk(conv_w), by_chunk(conv_b), by_chunk(ln_g), by_chunk(ln_b))


def _moba_kernel(q_ref, k_ref, v_ref, o_ref, kaug_ref, kmean_ref, *, blk, topk, key_tile):
    qi = pl.program_id(2)
    t_len = k_ref.shape[1]
    nb = t_len // blk
    heads = [slice(h * HEAD_DIM, (h + 1) * HEAD_DIM) for h in range(q_ref.shape[2] // HEAD_DIM)]
    c2 = HEAD_DIM ** -0.5 * LOG2_E

    @pl.when(qi == 0)
    def _():
        row = lax.broadcasted_iota(jnp.int32, (t_len, LANES), 0)
        col = lax.broadcasted_iota(jnp.int32, (t_len, LANES), 1)
        block_id = jnp.where(row // blk == col, 1.0, 0.0).astype(BF16)
        kmean_ref[...] = jnp.zeros(kmean_ref.shape, BF16)
        for h, hd in enumerate(heads):
            k = k_ref[0, :, hd]
            kaug_ref[h, :, 0:HEAD_DIM] = k
            kaug_ref[h, :, HEAD_DIM:HEAD_DIM + LANES] = block_id
            kmean = jnp.mean(k.astype(F32).reshape(nb, blk, HEAD_DIM), axis=1)
            kmean_ref[h, 0:nb, :] = kmean.astype(BF16)

    own = pl.ds(pl.multiple_of(qi * blk, blk), blk)
    r_i = lax.broadcasted_iota(jnp.int32, (blk, blk), 0)
    c_i = lax.broadcasted_iota(jnp.int32, (blk, blk), 1)
    causal = c_i <= r_i
    col = lax.broadcasted_iota(jnp.int32, (blk, LANES), 1)
    col_f = col.astype(F32)
    past = col < qi

    def head_start(h):
        q = q_ref[0, :, heads[h]]
        gate = lax.dot_general(q, kmean_ref[h], _NT, preferred_element_type=F32)
        g = jnp.where(past, gate, -jnp.inf)
        allowed = jnp.zeros(gate.shape, jnp.bool_)
        for _ in range(topk):
            best = jnp.max(g, axis=1, keepdims=True)
            first = jnp.min(jnp.where(g == best, col_f, float(LANES)), axis=1, keepdims=True)
            pick = col_f == first
            allowed = jnp.logical_or(allowed, jnp.logical_and(pick, past))
            g = jnp.where(pick, -jnp.inf, g)
        penalty = jnp.where(allowed, 0.0, MASKED).astype(BF16)
        q_aug = jnp.concatenate([q, penalty], axis=1)
        s = lax.dot_general(q, k_ref[0, own, heads[h]], _NT, preferred_element_type=F32)
        s = jnp.where(causal, s, MASKED)
        m = jnp.max(s, axis=1, keepdims=True)
        p = jnp.exp2((s - m) * c2)
        l = jnp.sum(p, axis=1, keepdims=True)
        acc = jnp.dot(p.astype(BF16), v_ref[0, own, heads[h]], preferred_element_type=F32)
        return q_aug, (m, l, acc)

    started = [head_start(h) for h in range(len(heads))]
    q_augs = [qa for qa, _ in started]

    def body(j, carry):
        ks = pl.ds(pl.multiple_of(j * key_tile, key_tile), key_tile)
        out = []
        for h, (m, l, acc) in enumerate(carry):
            s = lax.dot_general(q_augs[h], kaug_ref[h, ks, :], _NT, preferred_element_type=F32)
            m_new = jnp.maximum(m, jnp.max(s, axis=1, keepdims=True))
            alpha = jnp.exp2((m - m_new) * c2)
            p = jnp.exp2((s - m_new) * c2)
            l = alpha * l + jnp.sum(p, axis=1, keepdims=True)
            acc = alpha * acc + jnp.dot(p.astype(BF16), v_ref[0, ks, heads[h]], preferred_element_type=F32)
            out.append((m_new, l, acc))
        return out

    n_tiles = (qi * blk + key_tile - 1) // key_tile
    final = lax.fori_loop(0, n_tiles, body, [st for _, st in started])
    for h, (m, l, acc) in enumerate(final):
        o_ref[0, :, heads[h]] = (acc / l).astype(o_ref.dtype)


def _moba(qkv, n_heads, blk, topk, key_tile, heads_per_step):
    bsz, t_len, _ = qkv.shape
    assert n_heads % heads_per_step == 0 and t_len % key_tile == 0 and key_tile % blk == 0
    groups = n_heads // heads_per_step
    cols = heads_per_step * HEAD_DIM
    return pl.pallas_call(
        functools.partial(_moba_kernel, blk=blk, topk=topk, key_tile=key_tile),
        grid=(bsz, groups, t_len // blk),
        in_specs=[pl.BlockSpec((1, blk, cols), lambda b, h, i: (b, i, h)),
                  pl.BlockSpec((1, t_len, cols), lambda b, h, i: (b, 0, groups + h)),
                  pl.BlockSpec((1, t_len, cols), lambda b, h, i: (b, 0, 2 * groups + h))],
        out_specs=pl.BlockSpec((1, blk, cols), lambda b, h, i: (b, i, h)),
        out_shape=jax.ShapeDtypeStruct((bsz, t_len, n_heads * HEAD_DIM), BF16),
        scratch_shapes=[pltpu.VMEM((heads_per_step, t_len, HEAD_DIM + LANES), BF16),
                        pltpu.VMEM((heads_per_step, LANES, HEAD_DIM), BF16)],
        compiler_params=_params("parallel", "parallel", "arbitrary"),
    )(qkv, qkv, qkv)


def _sb_kernel(q_ref, k_ref, v_ref, o_ref, *, tile):
    qi = pl.program_id(2)
    scale = HEAD_DIM ** -0.5
    heads = [slice(h * HEAD_DIM, (h + 1) * HEAD_DIM) for h in range(q_ref.shape[2] // HEAD_DIM)]
    neg_q = [-q_ref[0, :, hd] for hd in heads]
    r_i = lax.broadcasted_iota(jnp.int32, (tile, tile), 0)
    c_i = lax.broadcasted_iota(jnp.int32, (tile, tile), 1)
    later = jnp.where(r_i > c_i, 1.0, 0.0).astype(BF16)
    later2 = jnp.concatenate([later, later], axis=0)
    strict = c_i < r_i

    def tile_terms(h, j, diagonal):
        ks = pl.ds(pl.multiple_of(j * tile, tile), tile)
        nz = lax.dot_general(neg_q[h], k_ref[0, ks, heads[h]], _NT, preferred_element_type=F32) * scale
        log_1m = jnp.minimum(nz, 0.0) - jnp.log(1.0 + jnp.exp2(jnp.abs(nz) * -LOG2_E))
        if diagonal:
            log_1m = jnp.where(strict, log_1m, 0.0)
        hi = log_1m.astype(BF16)
        lo = (log_1m - hi.astype(F32)).astype(BF16)
        after = jnp.dot(jnp.concatenate([hi, lo], axis=1), later2, preferred_element_type=F32)
        return ks, log_1m - nz + after, jnp.sum(log_1m, axis=1, keepdims=True)

    def tile_update(h, terms, run, acc, diagonal):
        ks, log_w, total = terms
        a = jnp.exp(log_w + run)
        if diagonal:
            a = jnp.where(strict, a, 0.0)
        acc = acc + jnp.dot(a.astype(BF16), v_ref[0, ks, heads[h]], preferred_element_type=F32)
        return run + total, acc

    n_heads = len(heads)
    diag_terms = [tile_terms(h, qi, True) for h in range(n_heads)]
    prev_terms = [tile_terms(h, jnp.maximum(qi - 1, 0), False) for h in range(n_heads)]
    first_tile = jnp.where(qi > 0, 0.0, MASKED)
    runs, accs = [], []
    for h in range(n_heads):
        run, acc = tile_update(h, diag_terms[h], jnp.zeros((tile, 1), F32), jnp.zeros((tile, HEAD_DIM), F32), True)
        run, acc = tile_update(h, prev_terms[h], run + first_tile, acc, False)
        runs.append(run)
        accs.append(acc)

    def live(runs):
        return jnp.max(functools.reduce(jnp.maximum, runs)) > SB_EXP_IS_ZERO

    def cond(carry):
        j, go, _, _ = carry
        return jnp.logical_and(j >= 0, go)

    def body(carry):
        j, _, runs, accs = carry
        out = [tile_update(h, tile_terms(h, j, False), runs[h], accs[h], False) for h in range(n_heads)]
        runs, accs = [r for r, _ in out], [a for _, a in out]
        return j - 1, live(runs), runs, accs

    _, _, runs, accs = lax.while_loop(cond, body, (qi - 2, live(runs), runs, accs))
    for h in range(n_heads):
        o_ref[0, :, heads[h]] = accs[h].astype(o_ref.dtype)


def _stick_breaking(qkv, n_heads, tile, heads_per_step):
    bsz, t_len, _ = qkv.shape
    assert n_heads % heads_per_step == 0 and t_len % tile == 0
    groups = n_heads // heads_per_step
    cols = heads_per_step * HEAD_DIM
    return pl.pallas_call(
        functools.partial(_sb_kernel, tile=tile),
        grid=(bsz, groups, t_len // tile),
        in_specs=[pl.BlockSpec((1, tile, cols), lambda b, h, i: (b, i, h)),
                  pl.BlockSpec((1, t_len, cols), lambda b, h, i: (b, 0, groups + h)),
                  pl.BlockSpec((1, t_len, cols), lambda b, h, i: (b, 0, 2 * groups + h))],
        out_specs=pl.BlockSpec((1, tile, cols), lambda b, h, i: (b, i, h)),
        out_shape=jax.ShapeDtypeStruct((bsz, t_len, n_heads * HEAD_DIM), BF16),
        compiler_params=_params("parallel", "parallel", "arbitrary"),
    )(qkv, qkv, qkv)


def _ffn_kernel(*refs, tm, tiles_per_seq, final, n_chains):
    if final:
        x_ref, g_ref, wu_ref, wg_ref, cw_ref, cb_ref, wd_ref, fg_ref, o_ref, h_ref, ubuf_ref, halo_ref = refs
    else:
        x_ref, g_ref, wu_ref, wg_ref, cw_ref, cb_ref, wd_ref, o_ref, h_ref, ubuf_ref, halo_ref = refs
    i = pl.program_id(0)
    f = pl.program_id(1)
    width = cw_ref.shape[0]

    @pl.when(f == 0)
    def _():
        x = x_ref[...]
        h_ref[...] = _rms(x, g_ref[...]).astype(BF16)
        o_ref[...] = x

        @pl.when(i == 0)
        def _():
            halo_ref[...] = jnp.zeros(halo_ref.shape, F32)

    h = h_ref[...]
    seq_start = (i % tiles_per_seq) == 0
    tf = wu_ref.shape[1]
    chunk = tf // n_chains
    acts = []
    for c in range(n_chains):
        cols = slice(c * chunk, (c + 1) * chunk)
        up = jnp.dot(h, wu_ref[:, cols], preferred_element_type=F32)
        gate = jnp.dot(h, wg_ref[:, cols], preferred_element_type=F32)
        ubuf_ref[0:SUBLANES, cols] = jnp.where(seq_start, 0.0, halo_ref[f, :, cols])
        ubuf_ref[SUBLANES:SUBLANES + tm, cols] = up
        halo_ref[f, :, cols] = up[tm - SUBLANES:tm, :]
        conv = up * cw_ref[width - 1:width, cols] + cb_ref[:, cols]
        for k in range(width - 1):
            conv = conv + ubuf_ref[pl.ds(SUBLANES - (width - 1) + k, tm), cols] * cw_ref[k:k + 1, cols]
        acts.append((conv * jax.nn.sigmoid(conv) * gate).astype(BF16))
    o_ref[...] += jnp.dot(jnp.concatenate(acts, axis=1), wd_ref[...], preferred_element_type=F32)

    if final:
        @pl.when(f == pl.num_programs(1) - 1)
        def _():
            o_ref[...] = _rms(o_ref[...], fg_ref[...])


def _ffn(x, g, w_up, w_gate, conv_w, conv_b, w_down, final_g, t_len, tm, tf, n_chains):
    n, d = x.shape
    d_ff = w_up.shape[1]
    width = conv_w.shape[0]
    final = final_g is not None
    row = lambda i, f: (i, 0)
    in_specs = [pl.BlockSpec((tm, d), row),
                pl.BlockSpec((1, d), lambda i, f: (0, 0)),
                pl.BlockSpec((d, tf), lambda i, f: (0, f)),
                pl.BlockSpec((d, tf), lambda i, f: (0, f)),
                pl.BlockSpec((width, tf), lambda i, f: (0, f)),
                pl.BlockSpec((1, tf), lambda i, f: (0, f)),
                pl.BlockSpec((tf, d), lambda i, f: (f, 0))]
    args = [x, g, w_up, w_gate, conv_w, conv_b, w_down]
    if final:
        in_specs.append(pl.BlockSpec((1, d), lambda i, f: (0, 0)))
        args.append(final_g)
    return pl.pallas_call(
        functools.partial(_ffn_kernel, tm=tm, tiles_per_seq=t_len // tm, final=final, n_chains=n_chains),
        grid=(n // tm, d_ff // tf),
        in_specs=in_specs,
        out_specs=pl.BlockSpec((tm, d), row),
        out_shape=jax.ShapeDtypeStruct((n, d), F32),
        scratch_shapes=[pltpu.VMEM((tm, d), BF16),
                        pltpu.VMEM((SUBLANES + tm, tf), F32),
                        pltpu.VMEM((d_ff // tf, SUBLANES, tf), F32)],
        compiler_params=_params("arbitrary", "arbitrary"),
    )(*args)


def kernel(x, mix_norm, ffn_norm, even_w_in, even_conv_w, even_conv_b, even_ln_g, even_ln_b, even_w_out,
           odd_w_qkv, odd_w_o, ffn_w_up, ffn_w_gate, ffn_conv_w, ffn_conv_b, ffn_w_down, final_norm):
    bsz, t_len, d = x.shape
    n = bsz * t_len
    depth = mix_norm.shape[0]
    xs = x.reshape(n, d)
    for layer in range(depth):
        j = layer // 2
        g = mix_norm[layer][None, :]
        if layer % 2 == 0:
            w_in = even_w_in[j]
            ag = _norm_matmul(xs, g, w_in[:, :2 * CONV_CH].astype(BF16), F32, 1024, 1024)
            qkv = _norm_matmul(xs, g, w_in[:, 2 * CONV_CH:].astype(BF16), BF16, 1024, 1024)
            y_a = _conformer(ag.reshape(bsz, t_len, -1), even_conv_w[j], even_conv_b[j][None, :],
                             even_ln_g[j][None, :], even_ln_b[j][None, :], 256)
            y_b = _moba(qkv.reshape(bsz, t_len, -1), MOBA_HEADS, MOBA_BLOCK, MOBA_TOPK, MOBA_KEY_TILE, MOBA_HEADS_PER_STEP)
            w_out = even_w_out[j].astype(BF16)
            xs = _matmul_res([y_a.reshape(n, -1), y_b.reshape(n, -1)], [w_out[:CONV_CH], w_out[CONV_CH:]],
                             xs, 1024, 1024)
        else:
            qkv = _norm_matmul(xs, g, odd_w_qkv[j].astype(BF16), BF16, 1024, 1024)
            y = _stick_breaking(qkv.reshape(bsz, t_len, -1), SB_HEADS, SB_TILE, SB_HEADS_PER_STEP)
            xs = _matmul_res([y.reshape(n, -1)], [odd_w_o[j].astype(BF16)], xs, 1024, 1024)
        final_g = final_norm[None, :] if layer == depth - 1 else None
        xs = _ffn(xs, ffn_norm[layer][None, :], ffn_w_up[layer].astype(BF16), ffn_w_gate[layer].astype(BF16),
                  ffn_conv_w[layer], ffn_conv_b[layer][None, :], ffn_w_down[layer].astype(BF16),
                  final_g, t_len, 512, 512, 2)
    return xs.reshape(bsz, t_len, d)
```

```python
import functools

import jax
import jax.numpy as jnp
from jax import lax
from jax.experimental import pallas as pl
from jax.experimental.pallas import tpu as pltpu

F32 = jnp.float32
BF16 = jnp.bfloat16

HEAD_DIM = 128
CONV_CH = 1024
CONV_WIDTH = 31
MOBA_HEADS = 8
MOBA_BLOCK = 256
MOBA_TOPK = 3
MOBA_KEY_TILE = 1024
MOBA_HEADS_PER_STEP = 4
SB_HEADS = 16
SB_TILE = 256
SB_HEADS_PER_STEP = 4
RMS_EPS = 1e-6
LN_EPS = 1e-5
LANES = 128
SUBLANES = 8
MASKED = -1e30
LOG2_E = 1.4426950408889634
SB_EXP_IS_ZERO = -105.0
VMEM_LIMIT = 58 * 1024 * 1024

_NT = (((1,), (1,)), ((), ()))


def _params(*semantics):
    return pltpu.CompilerParams(dimension_semantics=semantics, vmem_limit_bytes=VMEM_LIMIT)


def _rms(x, g):
    ms = jnp.mean(x * x, axis=-1, keepdims=True)
    return x * lax.rsqrt(ms + RMS_EPS) * g


def _norm_matmul_kernel(x_ref, g_ref, w_ref, o_ref, h_ref):
    @pl.when(pl.program_id(1) == 0)
    def _():
        h_ref[...] = _rms(x_ref[...], g_ref[...]).astype(BF16)

    o_ref[...] = jnp.dot(h_ref[...], w_ref[...], preferred_element_type=F32).astype(o_ref.dtype)


def _norm_matmul(x, g, w, out_dtype, tm, tn, col_start=0, col_stop=None):
    n, d = x.shape
    col_stop = w.shape[1] if col_stop is None else col_stop
    m = col_stop - col_start
    assert col_start % tn == 0 and m % tn == 0 and n % tm == 0
    first = col_start // tn
    return pl.pallas_call(
        _norm_matmul_kernel,
        grid=(n // tm, m // tn),
        in_specs=[pl.BlockSpec((tm, d), lambda i, j: (i, 0)),
                  pl.BlockSpec((1, d), lambda i, j: (0, 0)),
                  pl.BlockSpec((d, tn), lambda i, j: (0, first + j))],
        out_specs=pl.BlockSpec((tm, tn), lambda i, j: (i, j)),
        out_shape=jax.ShapeDtypeStruct((n, m), out_dtype),
        scratch_shapes=[pltpu.VMEM((tm, d), BF16)],
        compiler_params=_params("parallel", "arbitrary"),
    )(x, g, w)


def _matmul_res_kernel(*refs, n_pairs):
    a_refs, w_refs = refs[:n_pairs], refs[n_pairs:2 * n_pairs]
    x_ref, o_ref = refs[2 * n_pairs], refs[2 * n_pairs + 1]
    acc = x_ref[...]
    for a_ref, w_ref in zip(a_refs, w_refs):
        acc = acc + jnp.dot(a_ref[...], w_ref[...], preferred_element_type=F32)
    o_ref[...] = acc


def _matmul_res(a_list, w, x, tm, tn):
    n, m = x.shape
    n_pairs = len(a_list)
    widths = [a.shape[1] for a in a_list]
    assert all(k == widths[0] for k in widths) and sum(widths) == w.shape[0]
    in_specs = ([pl.BlockSpec((tm, k), lambda i, j: (i, 0)) for k in widths]
                + [pl.BlockSpec((k, tn), functools.partial(lambda i, j, p: (p, j), p=p)) for p, k in enumerate(widths)]
                + [pl.BlockSpec((tm, tn), lambda i, j: (i, j))])
    return pl.pallas_call(
        functools.partial(_matmul_res_kernel, n_pairs=n_pairs),
        grid=(n // tm, m // tn),
        in_specs=in_specs,
        out_specs=pl.BlockSpec((tm, tn), lambda i, j: (i, j)),
        out_shape=jax.ShapeDtypeStruct((n, m), F32),
        compiler_params=_params("parallel", "arbitrary"),
    )(*a_list, *([w] * n_pairs), x)


def _conformer_kernel(a_ref, g_ref, w_ref, b_ref, lng_ref, lnb_ref, o_ref, buf_ref, hs_ref, *, tt, halo):
    n_chunks = buf_ref.shape[0]
    width = w_ref.shape[1]

    @pl.when(pl.program_id(1) == 0)
    def _():
        buf_ref[:, 0:halo, :] = jnp.zeros((n_chunks, halo, LANES), F32)

    glu = a_ref[0] * jax.nn.sigmoid(g_ref[0])
    for c in range(n_chunks):
        buf_ref[c, halo:halo + tt, :] = glu[:, c * LANES:(c + 1) * LANES]

    def chunk_body(c, carry):
        acc = jnp.broadcast_to(b_ref[c], (tt, LANES))
        for k in range(width):
            acc = acc + buf_ref[c, pl.ds(halo - (width - 1) + k, tt), :] * w_ref[c, pl.ds(k, 1), :]
        hs_ref[c] = acc
        buf_ref[c, 0:halo, :] = buf_ref[c, tt:tt + halo, :]
        return carry

    lax.fori_loop(0, n_chunks, chunk_body, 0)

    channels = n_chunks * LANES
    total = hs_ref[0]
    for c in range(1, n_chunks):
        total = total + hs_ref[c]
    mu = jnp.sum(total, axis=-1, keepdims=True) * (1.0 / channels)
    sq = jnp.zeros((tt, LANES), F32)
    for c in range(n_chunks):
        d = hs_ref[c] - mu
        sq = sq + d * d
    var = jnp.sum(sq, axis=-1, keepdims=True) * (1.0 / channels)
    inv = lax.rsqrt(var + LN_EPS)
    for c in range(n_chunks):
        y = (hs_ref[c] - mu) * inv * lng_ref[c] + lnb_ref[c]
        o_ref[0, :, c * LANES:(c + 1) * LANES] = (y * jax.nn.sigmoid(y)).astype(o_ref.dtype)


def _conformer(ag, conv_w, conv_b, ln_g, ln_b, tt):
    bsz, t_len, c2 = ag.shape
    ch = c2 // 2
    n_chunks = ch // LANES
    width = conv_w.shape[0]
    halo = -(-(width - 1) // SUBLANES) * SUBLANES
    by_chunk = lambda p: p.reshape(-1, n_chunks, LANES).transpose(1, 0, 2)
    small = lambda rows: pl.BlockSpec((n_chunks, rows, LANES), lambda b, t: (0, 0, 0))
    return pl.pallas_call(
        functools.partial(_conformer_kernel, tt=tt, halo=halo),
        grid=(bsz, t_len // tt),
        in_specs=[pl.BlockSpec((1, tt, ch), lambda b, t: (b, t, 0)),
                  pl.BlockSpec((1, tt, ch), lambda b, t: (b, t, 1)),
                  small(width), small(1), small(1), small(1)],
        out_specs=pl.BlockSpec((1, tt, ch), lambda b, t: (b, t, 0)),
        out_shape=jax.ShapeDtypeStruct((bsz, t_len, ch), BF16),
        scratch_shapes=[pltpu.VMEM((n_chunks, halo + tt, LANES), F32),
                        pltpu.VMEM((n_chunks, tt, LANES), F32)],
        compiler_params=_params("parallel", "arbitrary"),
    )(ag, ag, by_chunk(conv_w), by_chunk(conv_b), by_chunk(ln_g), by_chunk(ln_b))


def _moba_kernel(q_ref, k_ref, v_ref, o_ref, kaug_ref, kmean_ref, *, blk, topk, key_tile):
    qi = pl.program_id(2)
    t_len = k_ref.shape[1]
    nb = t_len // blk
    heads = [slice(h * HEAD_DIM, (h + 1) * HEAD_DIM) for h in range(q_ref.shape[2] // HEAD_DIM)]
    c2 = HEAD_DIM ** -0.5 * LOG2_E

    @pl.when(qi == 0)
    def _():
        row = lax.broadcasted_iota(jnp.int32, (t_len, LANES), 0)
        col = lax.broadcasted_iota(jnp.int32, (t_len, LANES), 1)
        block_id = jnp.where(row // blk == col, 1.0, 0.0).astype(BF16)
        for h, hd in enumerate(heads):
            k = k_ref[0, :, hd]
            kaug_ref[h, :, 0:HEAD_DIM] = k
            kaug_ref[h, :, HEAD_DIM:HEAD_DIM + LANES] = block_id
            kmean = jnp.mean(k.astype(F32).reshape(nb, blk, HEAD_DIM), axis=1)
            kmean_ref[h] = kmean.astype(BF16)

    own = pl.ds(pl.multiple_of(qi * blk, blk), blk)
    r_i = lax.broadcasted_iota(jnp.int32, (blk, blk), 0)
    c_i = lax.broadcasted_iota(jnp.int32, (blk, blk), 1)
    causal = c_i <= r_i
    blk_id = lax.broadcasted_iota(jnp.int32, (nb, blk), 0)
    blk_id_f = blk_id.astype(F32)
    past = blk_id < qi

    def head_start(h):
        q = q_ref[0, :, heads[h]]
        gate = lax.dot_general(kmean_ref[h], q, _NT, preferred_element_type=F32)
        g = jnp.where(past, gate, -jnp.inf)
        allowed = jnp.zeros(gate.shape, jnp.bool_)
        for _ in range(topk):
            best = jnp.max(g, axis=0, keepdims=True)
            first = jnp.min(jnp.where(g == best, blk_id_f, float(nb)), axis=0, keepdims=True)
            pick = blk_id_f == first
            allowed = jnp.logical_or(allowed, jnp.logical_and(pick, past))
            g = jnp.where(pick, -jnp.inf, g)
        penalty_t = jnp.concatenate([jnp.where(allowed, 0.0, MASKED), jnp.zeros((LANES - nb, blk), F32)], axis=0)
        q_aug = jnp.concatenate([q, penalty_t.T.astype(BF16)], axis=1)
        s = lax.dot_general(q, k_ref[0, own, heads[h]], _NT, preferred_element_type=F32)
        s = jnp.where(causal, s, MASKED)
        m = jnp.max(s, axis=1, keepdims=True)
        p = jnp.exp2((s - m) * c2)
        l = jnp.sum(p, axis=1, keepdims=True)
        acc = jnp.dot(p.astype(BF16), v_ref[0, own, heads[h]], preferred_element_type=F32)
        return q_aug, (m, l, acc)

    started = [head_start(h) for h in range(len(heads))]
    q_augs = [qa for qa, _ in started]

    def body(j, carry):
        ks = pl.ds(pl.multiple_of(j * key_tile, key_tile), key_tile)
        out = []
        for h, (m, l, acc) in enumerate(carry):
            s = lax.dot_general(q_augs[h], kaug_ref[h, ks, :], _NT, preferred_element_type=F32)
            m_new = jnp.maximum(m, jnp.max(s, axis=1, keepdims=True))
            alpha = jnp.exp2((m - m_new) * c2)
            p = jnp.exp2((s - m_new) * c2)
            l = alpha * l + jnp.sum(p, axis=1, keepdims=True)
            acc = alpha * acc + jnp.dot(p.astype(BF16), v_ref[0, ks, heads[h]], preferred_element_type=F32)
            out.append((m_new, l, acc))
        return out

    n_tiles = (qi * blk + key_tile - 1) // key_tile
    final = lax.fori_loop(0, n_tiles, body, [st for _, st in started])
    for h, (m, l, acc) in enumerate(final):
        o_ref[0, :, heads[h]] = (acc / l).astype(o_ref.dtype)


def _moba(qkv, n_heads, blk, topk, key_tile, heads_per_step):
    bsz, t_len, _ = qkv.shape
    assert n_heads % heads_per_step == 0 and t_len % key_tile == 0 and key_tile % blk == 0
    assert (t_len // blk) % SUBLANES == 0 and t_len // blk <= LANES
    groups = n_heads // heads_per_step
    cols = heads_per_step * HEAD_DIM
    return pl.pallas_call(
        functools.partial(_moba_kernel, blk=blk, topk=topk, key_tile=key_tile),
        grid=(bsz, groups, t_len // blk),
        in_specs=[pl.BlockSpec((1, blk, cols), lambda b, h, i: (b, i, h)),
                  pl.BlockSpec((1, t_len, cols), lambda b, h, i: (b, 0, groups + h)),
                  pl.BlockSpec((1, t_len, cols), lambda b, h, i: (b, 0, 2 * groups + h))],
        out_specs=pl.BlockSpec((1, blk, cols), lambda b, h, i: (b, i, h)),
        out_shape=jax.ShapeDtypeStruct((bsz, t_len, n_heads * HEAD_DIM), BF16),
        scratch_shapes=[pltpu.VMEM((heads_per_step, t_len, HEAD_DIM + LANES), BF16),
                        pltpu.VMEM((heads_per_step, t_len // blk, HEAD_DIM), BF16)],
        compiler_params=_params("parallel", "parallel", "arbitrary"),
    )(qkv, qkv, qkv)


def _sb_kernel(q_ref, k_ref, v_ref, o_ref, *, tile):
    qi = pl.program_id(2)
    scale = HEAD_DIM ** -0.5
    heads = [slice(h * HEAD_DIM, (h + 1) * HEAD_DIM) for h in range(q_ref.shape[2] // HEAD_DIM)]
    neg_q = [-q_ref[0, :, hd] for hd in heads]
    r_i = lax.broadcasted_iota(jnp.int32, (tile, tile), 0)
    c_i = lax.broadcasted_iota(jnp.int32, (tile, tile), 1)
    later = jnp.where(r_i > c_i, 1.0, 0.0).astype(BF16)
    later2 = jnp.concatenate([later, later], axis=0)
    strict = c_i < r_i

    def tile_terms(h, j, diagonal):
        ks = pl.ds(pl.multiple_of(j * tile, tile), tile)
        nz = lax.dot_general(neg_q[h], k_ref[0, ks, heads[h]], _NT, preferred_element_type=F32) * scale
        log_1m = jnp.minimum(nz, 0.0) - jnp.log(1.0 + jnp.exp2(jnp.abs(nz) * -LOG2_E))
        if diagonal:
            log_1m = jnp.where(strict, log_1m, 0.0)
        hi = log_1m.astype(BF16)
        lo = (log_1m - hi.astype(F32)).astype(BF16)
        after = jnp.dot(jnp.concatenate([hi, lo], axis=1), later2, preferred_element_type=F32)
        return ks, log_1m - nz + after, jnp.sum(log_1m, axis=1, keepdims=True)

    def tile_update(h, terms, run, acc, diagonal):
        ks, log_w, total = terms
        a = jnp.exp(log_w + run)
        if diagonal:
            a = jnp.where(strict, a, 0.0)
        acc = acc + jnp.dot(a.astype(BF16), v_ref[0, ks, heads[h]], preferred_element_type=F32)
        return run + total, acc

    n_heads = len(heads)
    diag_terms = [tile_terms(h, qi, True) for h in range(n_heads)]
    prev_terms = [tile_terms(h, jnp.maximum(qi - 1, 0), False) for h in range(n_heads)]
    first_tile = jnp.where(qi > 0, 0.0, MASKED)
    runs, accs = [], []
    for h in range(n_heads):
        run, acc = tile_update(h, diag_terms[h], jnp.zeros((tile, 1), F32), jnp.zeros((tile, HEAD_DIM), F32), True)
        run, acc = tile_update(h, prev_terms[h], run + first_tile, acc, False)
        runs.append(run)
        accs.append(acc)

    def live(runs):
        return jnp.max(functools.reduce(jnp.maximum, runs)) > SB_EXP_IS_ZERO

    def cond(carry):
        j, go, _, _ = carry
        return jnp.logical_and(j >= 0, go)

    def body(carry):
        j, _, runs, accs = carry
        out = [tile_update(h, tile_terms(h, j, False), runs[h], accs[h], False) for h in range(n_heads)]
        runs, accs = [r for r, _ in out], [a for _, a in out]
        return j - 1, live(runs), runs, accs

    _, _, runs, accs = lax.while_loop(cond, body, (qi - 2, live(runs), runs, accs))
    for h in range(n_heads):
        o_ref[0, :, heads[h]] = accs[h].astype(o_ref.dtype)


def _stick_breaking(qkv, n_heads, tile, heads_per_step):
    bsz, t_len, _ = qkv.shape
    assert n_heads % heads_per_step == 0 and t_len % tile == 0
    groups = n_heads // heads_per_step
    cols = heads_per_step * HEAD_DIM
    return pl.pallas_call(
        functools.partial(_sb_kernel, tile=tile),
        grid=(bsz, groups, t_len // tile),
        in_specs=[pl.BlockSpec((1, tile, cols), lambda b, h, i: (b, i, h)),
                  pl.BlockSpec((1, t_len, cols), lambda b, h, i: (b, 0, groups + h)),
                  pl.BlockSpec((1, t_len, cols), lambda b, h, i: (b, 0, 2 * groups + h))],
        out_specs=pl.BlockSpec((1, tile, cols), lambda b, h, i: (b, i, h)),
        out_shape=jax.ShapeDtypeStruct((bsz, t_len, n_heads * HEAD_DIM), BF16),
        compiler_params=_params("parallel", "parallel", "arbitrary"),
    )(qkv, qkv, qkv)


def _ffn_kernel(*refs, tm, tiles_per_seq, final, n_chains):
    if final:
        x_ref, g_ref, wu_ref, wg_ref, cw_ref, cb_ref, wd_ref, fg_ref, o_ref, h_ref, ubuf_ref, halo_ref = refs
    else:
        x_ref, g_ref, wu_ref, wg_ref, cw_ref, cb_ref, wd_ref, o_ref, h_ref, ubuf_ref, halo_ref = refs
    i = pl.program_id(0)
    f = pl.program_id(1)
    width = cw_ref.shape[0]

    @pl.when(f == 0)
    def _():
        x = x_ref[...]
        h_ref[...] = _rms(x, g_ref[...]).astype(BF16)
        o_ref[...] = x

        @pl.when(i == 0)
        def _():
            halo_ref[...] = jnp.zeros(halo_ref.shape, F32)

    h = h_ref[...]
    seq_start = (i % tiles_per_seq) == 0
    tf = wu_ref.shape[1]
    chunk = tf // n_chains
    acts = []
    for c in range(n_chains):
        cols = slice(c * chunk, (c + 1) * chunk)
        up = jnp.dot(h, wu_ref[:, cols], preferred_element_type=F32)
        gate = jnp.dot(h, wg_ref[:, cols], preferred_element_type=F32)
        ubuf_ref[0:SUBLANES, cols] = jnp.where(seq_start, 0.0, halo_ref[f, :, cols])
        ubuf_ref[SUBLANES:SUBLANES + tm, cols] = up
        halo_ref[f, :, cols] = up[tm - SUBLANES:tm, :]
        conv = up * cw_ref[width - 1:width, cols] + cb_ref[:, cols]
        for k in range(width - 1):
            conv = conv + ubuf_ref[pl.ds(SUBLANES - (width - 1) + k, tm), cols] * cw_ref[k:k + 1, cols]
        acts.append((conv * jax.nn.sigmoid(conv) * gate).astype(BF16))
    o_ref[...] += jnp.dot(jnp.concatenate(acts, axis=1), wd_ref[...], preferred_element_type=F32)

    if final:
        @pl.when(f == pl.num_programs(1) - 1)
        def _():
            o_ref[...] = _rms(o_ref[...], fg_ref[...])


def _ffn(x, g, w_up, w_gate, conv_w, conv_b, w_down, final_g, t_len, tm, tf, n_chains):
    n, d = x.shape
    d_ff = w_up.shape[1]
    width = conv_w.shape[0]
    final = final_g is not None
    row = lambda i, f: (i, 0)
    in_specs = [pl.BlockSpec((tm, d), row),
                pl.BlockSpec((1, d), lambda i, f: (0, 0)),
                pl.BlockSpec((d, tf), lambda i, f: (0, f)),
                pl.BlockSpec((d, tf), lambda i, f: (0, f)),
                pl.BlockSpec((width, tf), lambda i, f: (0, f)),
                pl.BlockSpec((1, tf), lambda i, f: (0, f)),
                pl.BlockSpec((tf, d), lambda i, f: (f, 0))]
    args = [x, g, w_up, w_gate, conv_w, conv_b, w_down]
    if final:
        in_specs.append(pl.BlockSpec((1, d), lambda i, f: (0, 0)))
        args.append(final_g)
    return pl.pallas_call(
        functools.partial(_ffn_kernel, tm=tm, tiles_per_seq=t_len // tm, final=final, n_chains=n_chains),
        grid=(n // tm, d_ff // tf),
        in_specs=in_specs,
        out_specs=pl.BlockSpec((tm, d), row),
        out_shape=jax.ShapeDtypeStruct((n, d), F32),
        scratch_shapes=[pltpu.VMEM((tm, d), BF16),
                        pltpu.VMEM((SUBLANES + tm, tf), F32),
                        pltpu.VMEM((d_ff // tf, SUBLANES, tf), F32)],
        compiler_params=_params("arbitrary", "arbitrary"),
    )(*args)


def kernel(x, mix_norm, ffn_norm, even_w_in, even_conv_w, even_conv_b, even_ln_g, even_ln_b, even_w_out,
           odd_w_qkv, odd_w_o, ffn_w_up, ffn_w_gate, ffn_conv_w, ffn_conv_b, ffn_w_down, final_norm):
    bsz, t_len, d = x.shape
    n = bsz * t_len
    depth = mix_norm.shape[0]
    xs = x.reshape(n, d)
    for layer in range(depth):
        j = layer // 2
        g = mix_norm[layer][None, :]
        if layer % 2 == 0:
            w_in = even_w_in[j].astype(BF16)
            ag = _norm_matmul(xs, g, w_in, F32, 1024, 1024, 0, 2 * CONV_CH)
            qkv = _norm_matmul(xs, g, w_in, BF16, 1024, 1024, 2 * CONV_CH).reshape(bsz, t_len, -1)
            y_a = _conformer(ag.reshape(bsz, t_len, -1), even_conv_w[j], even_conv_b[j][None, :],
                             even_ln_g[j][None, :], even_ln_b[j][None, :], 256)
            y_b = _moba(qkv, MOBA_HEADS, MOBA_BLOCK, MOBA_TOPK, MOBA_KEY_TILE, MOBA_HEADS_PER_STEP)
            xs = _matmul_res([y_a.reshape(n, -1), y_b.reshape(n, -1)], even_w_out[j].astype(BF16), xs, 1024, 1024)
        else:
            qkv = _norm_matmul(xs, g, odd_w_qkv[j].astype(BF16), BF16, 1024, 1024)
            y = _stick_breaking(qkv.reshape(bsz, t_len, -1), SB_HEADS, SB_TILE, SB_HEADS_PER_STEP)
            xs = _matmul_res([y.reshape(n, -1)], odd_w_o[j].astype(BF16), xs, 1024, 1024)
        final_g = final_norm[None, :] if layer == depth - 1 else None
        xs = _ffn(xs, ffn_norm[layer][None, :], ffn_w_up[layer].astype(BF16), ffn_w_gate[layer].astype(BF16),
                  ffn_conv_w[layer], ffn_conv_b[layer][None, :], ffn_w_down[layer].astype(BF16),
                  final_g, t_len, 1024, 512, 2)
    return xs.reshape(bsz, t_len, d)
```

```python
import functools

import jax
import jax.numpy as jnp
from jax import lax
from jax.experimental import pallas as pl
from jax.experimental.pallas import tpu as pltpu

F32 = jnp.float32
BF16 = jnp.bfloat16

HEAD_DIM = 128
CONV_CH = 1024
CONV_WIDTH = 31
MOBA_HEADS = 8
MOBA_BLOCK = 256
MOBA_TOPK = 3
MOBA_KEY_TILE = 1024
MOBA_HEADS_PER_STEP = 4
SB_HEADS = 16
SB_TILE = 256
SB_HEADS_PER_STEP = 4
RMS_EPS = 1e-6
LN_EPS = 1e-5
LANES = 128
SUBLANES = 8
MASKED = -1e30
LOG2_E = 1.4426950408889634
SB_EXP_IS_ZERO = -105.0
VMEM_LIMIT = 58 * 1024 * 1024

_NT = (((1,), (1,)), ((), ()))


def _params(*semantics):
    return pltpu.CompilerParams(dimension_semantics=semantics, vmem_limit_bytes=VMEM_LIMIT)


def _rms(x, g):
    ms = jnp.mean(x * x, axis=-1, keepdims=True)
    return x * lax.rsqrt(ms + RMS_EPS) * g


def _norm_matmul_kernel(x_ref, g_ref, w_ref, o_ref, h_ref):
    @pl.when(pl.program_id(1) == 0)
    def _():
        h_ref[...] = _rms(x_ref[...], g_ref[...]).astype(BF16)

    o_ref[...] = jnp.dot(h_ref[...], w_ref[...], preferred_element_type=F32).astype(o_ref.dtype)


def _norm_matmul(x, g, w, out_dtype, tm, tn, col_start=0, col_stop=None):
    n, d = x.shape
    col_stop = w.shape[1] if col_stop is None else col_stop
    m = col_stop - col_start
    assert col_start % tn == 0 and m % tn == 0 and n % tm == 0
    first = col_start // tn
    return pl.pallas_call(
        _norm_matmul_kernel,
        grid=(n // tm, m // tn),
        in_specs=[pl.BlockSpec((tm, d), lambda i, j: (i, 0)),
                  pl.BlockSpec((1, d), lambda i, j: (0, 0)),
                  pl.BlockSpec((d, tn), lambda i, j: (0, first + j))],
        out_specs=pl.BlockSpec((tm, tn), lambda i, j: (i, j)),
        out_shape=jax.ShapeDtypeStruct((n, m), out_dtype),
        scratch_shapes=[pltpu.VMEM((tm, d), BF16)],
        compiler_params=_params("parallel", "arbitrary"),
    )(x, g, w)


def _matmul_res_kernel(*refs, n_pairs):
    a_refs, w_refs = refs[:n_pairs], refs[n_pairs:2 * n_pairs]
    x_ref, o_ref = refs[2 * n_pairs], refs[2 * n_pairs + 1]
    acc = x_ref[...]
    for a_ref, w_ref in zip(a_refs, w_refs):
        acc = acc + jnp.dot(a_ref[...], w_ref[...], preferred_element_type=F32)
    o_ref[...] = acc


def _matmul_res(a_list, w, x, tm, tn):
    n, m = x.shape
    n_pairs = len(a_list)
    widths = [a.shape[1] for a in a_list]
    assert all(k == widths[0] for k in widths) and sum(widths) == w.shape[0]
    in_specs = ([pl.BlockSpec((tm, k), lambda i, j: (i, 0)) for k in widths]
                + [pl.BlockSpec((k, tn), functools.partial(lambda i, j, p: (p, j), p=p)) for p, k in enumerate(widths)]
                + [pl.BlockSpec((tm, tn), lambda i, j: (i, j))])
    return pl.pallas_call(
        functools.partial(_matmul_res_kernel, n_pairs=n_pairs),
        grid=(n // tm, m // tn),
        in_specs=in_specs,
        out_specs=pl.BlockSpec((tm, tn), lambda i, j: (i, j)),
        out_shape=jax.ShapeDtypeStruct((n, m), F32),
        compiler_params=_params("parallel", "arbitrary"),
    )(*a_list, *([w] * n_pairs), x)


def _conformer_kernel(a_ref, g_ref, w_ref, b_ref, lng_ref, lnb_ref, o_ref, buf_ref, hs_ref, *, tt, halo):
    n_chunks = buf_ref.shape[0]
    width = w_ref.shape[1]

    @pl.when(pl.program_id(1) == 0)
    def _():
        buf_ref[:, 0:halo, :] = jnp.zeros((n_chunks, halo, LANES), F32)

    glu = a_ref[0] * jax.nn.sigmoid(g_ref[0])
    for c in range(n_chunks):
        buf_ref[c, halo:halo + tt, :] = glu[:, c * LANES:(c + 1) * LANES]

    def chunk_body(c, carry):
        acc = jnp.broadcast_to(b_ref[c], (tt, LANES))
        for k in range(width):
            acc = acc + buf_ref[c, pl.ds(halo - (width - 1) + k, tt), :] * w_ref[c, pl.ds(k, 1), :]
        hs_ref[c] = acc
        buf_ref[c, 0:halo, :] = buf_ref[c, tt:tt + halo, :]
        return carry

    lax.fori_loop(0, n_chunks, chunk_body, 0)

    channels = n_chunks * LANES
    total = hs_ref[0]
    for c in range(1, n_chunks):
        total = total + hs_ref[c]
    mu = jnp.sum(total, axis=-1, keepdims=True) * (1.0 / channels)
    sq = jnp.zeros((tt, LANES), F32)
    for c in range(n_chunks):
        d = hs_ref[c] - mu
        sq = sq + d * d
    var = jnp.sum(sq, axis=-1, keepdims=True) * (1.0 / channels)
    inv = lax.rsqrt(var + LN_EPS)
    for c in range(n_chunks):
        y = (hs_ref[c] - mu) * inv * lng_ref[c] + lnb_ref[c]
        o_ref[0, :, c * LANES:(c + 1) * LANES] = (y * jax.nn.sigmoid(y)).astype(o_ref.dtype)


def _conformer(ag, conv_w, conv_b, ln_g, ln_b, tt):
    bsz, t_len, c2 = ag.shape
    ch = c2 // 2
    n_chunks = ch // LANES
    width = conv_w.shape[0]
    halo = -(-(width - 1) // SUBLANES) * SUBLANES
    by_chunk = lambda p: p.reshape(-1, n_chunks, LANES).transpose(1, 0, 2)
    small = lambda rows: pl.BlockSpec((n_chunks, rows, LANES), lambda b, t: (0, 0, 0))
    return pl.pallas_call(
        functools.partial(_conformer_kernel, tt=tt, halo=halo),
        grid=(bsz, t_len // tt),
        in_specs=[pl.BlockSpec((1, tt, ch), lambda b, t: (b, t, 0)),
                  pl.BlockSpec((1, tt, ch), lambda b, t: (b, t, 1)),
                  small(width), small(1), small(1), small(1)],
        out_specs=pl.BlockSpec((1, tt, ch), lambda b, t: (b, t, 0)),
        out_shape=jax.ShapeDtypeStruct((bsz, t_len, ch), BF16),
        scratch_shapes=[pltpu.VMEM((n_chunks, halo + tt, LANES), F32),
                        pltpu.VMEM((n_chunks, tt, LANES), F32)],
        compiler_params=_params("parallel", "arbitrary"),
    )(ag, ag, by_chunk(conv_w), by_chunk(conv_b), by_chunk(ln_g), by_chunk(ln_b))


def _moba_kernel(q_ref, k_ref, v_ref, o_ref, kaug_ref, kmean_ref, *, blk, topk, key_tile):
    qi = pl.program_id(2)
    t_len = k_ref.shape[1]
    nb = t_len // blk
    heads = [slice(h * HEAD_DIM, (h + 1) * HEAD_DIM) for h in range(q_ref.shape[2] // HEAD_DIM)]
    c2 = HEAD_DIM ** -0.5 * LOG2_E

    @pl.when(qi == 0)
    def _():
        row = lax.broadcasted_iota(jnp.int32, (t_len, LANES), 0)
        col = lax.broadcasted_iota(jnp.int32, (t_len, LANES), 1)
        block_id = jnp.where(row // blk == col, 1.0, 0.0).astype(BF16)
        for h, hd in enumerate(heads):
            k = k_ref[0, :, hd]
            kaug_ref[h, :, 0:HEAD_DIM] = k
            kaug_ref[h, :, HEAD_DIM:HEAD_DIM + LANES] = block_id
            kmean = jnp.mean(k.astype(F32).reshape(nb, blk, HEAD_DIM), axis=1)
            kmean_ref[h] = kmean.astype(BF16)

    own = pl.ds(pl.multiple_of(qi * blk, blk), blk)
    r_i = lax.broadcasted_iota(jnp.int32, (blk, blk), 0)
    c_i = lax.broadcasted_iota(jnp.int32, (blk, blk), 1)
    causal = c_i <= r_i
    blk_id = lax.broadcasted_iota(jnp.int32, (nb, blk), 0)
    blk_id_f = blk_id.astype(F32)
    past = blk_id < qi

    def head_start(h):
        q = q_ref[0, :, heads[h]]
        gate = lax.dot_general(kmean_ref[h], q, _NT, preferred_element_type=F32)
        g = jnp.where(past, gate, -jnp.inf)
        allowed = jnp.zeros(gate.shape, jnp.bool_)
        for _ in range(topk):
            best = jnp.max(g, axis=0, keepdims=True)
            first = jnp.min(jnp.where(g == best, blk_id_f, float(nb)), axis=0, keepdims=True)
            pick = blk_id_f == first
            allowed = jnp.logical_or(allowed, jnp.logical_and(pick, past))
            g = jnp.where(pick, -jnp.inf, g)
        penalty_t = jnp.concatenate([jnp.where(allowed, 0.0, MASKED), jnp.zeros((LANES - nb, blk), F32)], axis=0)
        q_aug = jnp.concatenate([q, penalty_t.T.astype(BF16)], axis=1)
        s = lax.dot_general(q, k_ref[0, own, heads[h]], _NT, preferred_element_type=F32)
        s = jnp.where(causal, s, MASKED)
        m = jnp.max(s, axis=1, keepdims=True)
        p = jnp.exp2((s - m) * c2)
        l = jnp.sum(p, axis=1, keepdims=True)
        acc = jnp.dot(p.astype(BF16), v_ref[0, own, heads[h]], preferred_element_type=F32)
        return q_aug, (m, l, acc)

    started = [head_start(h) for h in range(len(heads))]
    q_augs = [qa for qa, _ in started]

    def body(j, carry):
        ks = pl.ds(pl.multiple_of(j * key_tile, key_tile), key_tile)
        out = []
        for h, (m, l, acc) in enumerate(carry):
            s = lax.dot_general(q_augs[h], kaug_ref[h, ks, :], _NT, preferred_element_type=F32)
            m_new = jnp.maximum(m, jnp.max(s, axis=1, keepdims=True))
            alpha = jnp.exp2((m - m_new) * c2)
            p = jnp.exp2((s - m_new) * c2)
            l = alpha * l + jnp.sum(p, axis=1, keepdims=True)
            acc = alpha * acc + jnp.dot(p.astype(BF16), v_ref[0, ks, heads[h]], preferred_element_type=F32)
            out.append((m_new, l, acc))
        return out

    n_tiles = (qi * blk + key_tile - 1) // key_tile
    final = lax.fori_loop(0, n_tiles, body, [st for _, st in started])
    for h, (m, l, acc) in enumerate(final):
        o_ref[0, :, heads[h]] = (acc / l).astype(o_ref.dtype)


def _moba(qkv, n_heads, blk, topk, key_tile, heads_per_step):
    bsz, t_len, _ = qkv.shape
    assert n_heads % heads_per_step == 0 and t_len % key_tile == 0 and key_tile % blk == 0
    assert (t_len // blk) % SUBLANES == 0 and t_len // blk <= LANES
    groups = n_heads // heads_per_step
    cols = heads_per_step * HEAD_DIM
    return pl.pallas_call(
        functools.partial(_moba_kernel, blk=blk, topk=topk, key_tile=key_tile),
        grid=(bsz, groups, t_len // blk),
        in_specs=[pl.BlockSpec((1, blk, cols), lambda b, h, i: (b, i, h)),
                  pl.BlockSpec((1, t_len, cols), lambda b, h, i: (b, 0, groups + h)),
                  pl.BlockSpec((1, t_len, cols), lambda b, h, i: (b, 0, 2 * groups + h))],
        out_specs=pl.BlockSpec((1, blk, cols), lambda b, h, i: (b, i, h)),
        out_shape=jax.ShapeDtypeStruct((bsz, t_len, n_heads * HEAD_DIM), BF16),
        scratch_shapes=[pltpu.VMEM((heads_per_step, t_len, HEAD_DIM + LANES), BF16),
                        pltpu.VMEM((heads_per_step, t_len // blk, HEAD_DIM), BF16)],
        compiler_params=_params("parallel", "parallel", "arbitrary"),
    )(qkv, qkv, qkv)


def _sb_kernel(q_ref, k_ref, v_ref, o_ref, *, tile):
    qi = pl.program_id(2)
    scale = HEAD_DIM ** -0.5
    heads = [slice(h * HEAD_DIM, (h + 1) * HEAD_DIM) for h in range(q_ref.shape[2] // HEAD_DIM)]
    neg_q = [-q_ref[0, :, hd] for hd in heads]
    r_i = lax.broadcasted_iota(jnp.int32, (tile, tile), 0)
    c_i = lax.broadcasted_iota(jnp.int32, (tile, tile), 1)
    later = jnp.where(r_i > c_i, 1.0, 0.0).astype(BF16)
    later2 = jnp.concatenate([later, later], axis=0)
    strict = c_i < r_i

    def tile_terms(h, j, diagonal):
        ks = pl.ds(pl.multiple_of(j * tile, tile), tile)
        nz = lax.dot_general(neg_q[h], k_ref[0, ks, heads[h]], _NT, preferred_element_type=F32) * scale
        log_1m = jnp.minimum(nz, 0.0) - jnp.log(1.0 + jnp.exp2(jnp.abs(nz) * -LOG2_E))
        if diagonal:
            log_1m = jnp.where(strict, log_1m, 0.0)
        hi = log_1m.astype(BF16)
        lo = (log_1m - hi.astype(F32)).astype(BF16)
        after = jnp.dot(jnp.concatenate([hi, lo], axis=1), later2, preferred_element_type=F32)
        return ks, log_1m - nz + after, jnp.sum(log_1m, axis=1, keepdims=True)

    def tile_update(h, terms, run, acc, diagonal):
        ks, log_w, total = terms
        a = jnp.exp(log_w + run)
        if diagonal:
            a = jnp.where(strict, a, 0.0)
        acc = acc + jnp.dot(a.astype(BF16), v_ref[0, ks, heads[h]], preferred_element_type=F32)
        return run + total, acc

    n_heads = len(heads)
    diag_terms = [tile_terms(h, qi, True) for h in range(n_heads)]
    prev_terms = [tile_terms(h, jnp.maximum(qi - 1, 0), False) for h in range(n_heads)]
    first_tile = jnp.where(qi > 0, 0.0, MASKED)
    runs, accs = [], []
    for h in range(n_heads):
        run, acc = tile_update(h, diag_terms[h], jnp.zeros((tile, 1), F32), jnp.zeros((tile, HEAD_DIM), F32), True)
        run, acc = tile_update(h, prev_terms[h], run + first_tile, acc, False)
        runs.append(run)
        accs.append(acc)

    def live(runs):
        return jnp.max(functools.reduce(jnp.maximum, runs)) > SB_EXP_IS_ZERO

    def cond(carry):
        j, go, _, _ = carry
        return jnp.logical_and(j >= 0, go)

    def body(carry):
        j, _, runs, accs = carry
        out = [tile_update(h, tile_terms(h, j, False), runs[h], accs[h], False) for h in range(n_heads)]
        runs, accs = [r for r, _ in out], [a for _, a in out]
        return j - 1, live(runs), runs, accs

    _, _, runs, accs = lax.while_loop(cond, body, (qi - 2, live(runs), runs, accs))
    for h in range(n_heads):
        o_ref[0, :, heads[h]] = accs[h].astype(o_ref.dtype)


def _stick_breaking(qkv, n_heads, tile, heads_per_step):
    bsz, t_len, _ = qkv.shape
    assert n_heads % heads_per_step == 0 and t_len % tile == 0
    groups = n_heads // heads_per_step
    cols = heads_per_step * HEAD_DIM
    return pl.pallas_call(
        functools.partial(_sb_kernel, tile=tile),
        grid=(bsz, groups, t_len // tile),
        in_specs=[pl.BlockSpec((1, tile, cols), lambda b, h, i: (b, i, h)),
                  pl.BlockSpec((1, t_len, cols), lambda b, h, i: (b, 0, groups + h)),
                  pl.BlockSpec((1, t_len, cols), lambda b, h, i: (b, 0, 2 * groups + h))],
        out_specs=pl.BlockSpec((1, tile, cols), lambda b, h, i: (b, i, h)),
        out_shape=jax.ShapeDtypeStruct((bsz, t_len, n_heads * HEAD_DIM), BF16),
        compiler_params=_params("parallel", "parallel", "arbitrary"),
    )(qkv, qkv, qkv)


def _ffn_kernel(*refs, tm, tiles_per_seq, final, n_chains):
    if final:
        x_ref, g_ref, wu_ref, wg_ref, cw_ref, cb_ref, wd_ref, fg_ref, o_ref, h_ref, ubuf_ref, halo_ref = refs
    else:
        x_ref, g_ref, wu_ref, wg_ref, cw_ref, cb_ref, wd_ref, o_ref, h_ref, ubuf_ref, halo_ref = refs
    i = pl.program_id(0)
    f = pl.program_id(1)
    width = cw_ref.shape[0]

    @pl.when(f == 0)
    def _():
        x = x_ref[...]
        h_ref[...] = _rms(x, g_ref[...]).astype(BF16)
        o_ref[...] = x

        @pl.when(i == 0)
        def _():
            halo_ref[...] = jnp.zeros(halo_ref.shape, F32)

    h = h_ref[...]
    seq_start = (i % tiles_per_seq) == 0
    tf = wu_ref.shape[1]
    chunk = tf // n_chains
    acts = []
    for c in range(n_chains):
        cols = slice(c * chunk, (c + 1) * chunk)
        up = jnp.dot(h, wu_ref[:, cols], preferred_element_type=F32)
        gate = jnp.dot(h, wg_ref[:, cols], preferred_element_type=F32)
        ubuf_ref[0:SUBLANES, cols] = jnp.where(seq_start, 0.0, halo_ref[f, :, cols])
        ubuf_ref[SUBLANES:SUBLANES + tm, cols] = up
        halo_ref[f, :, cols] = up[tm - SUBLANES:tm, :]
        conv = up * cw_ref[width - 1:width, cols] + cb_ref[:, cols]
        for k in range(width - 1):
            conv = conv + ubuf_ref[pl.ds(SUBLANES - (width - 1) + k, tm), cols] * cw_ref[k:k + 1, cols]
        acts.append((conv * jax.nn.sigmoid(conv) * gate).astype(BF16))
    o_ref[...] += jnp.dot(jnp.concatenate(acts, axis=1), wd_ref[...], preferred_element_type=F32)

    if final:
        @pl.when(f == pl.num_programs(1) - 1)
        def _():
            o_ref[...] = _rms(o_ref[...], fg_ref[...])


def _ffn(x, g, layer, w_up, w_gate, conv_w, conv_b, w_down, final_g, t_len, tm, tf, n_chains):
    n, d = x.shape
    d_ff = w_up.shape[2]
    width = conv_w.shape[0]
    final = final_g is not None
    row = lambda i, f: (i, 0)
    in_specs = [pl.BlockSpec((tm, d), row),
                pl.BlockSpec((1, d), lambda i, f: (0, 0)),
                pl.BlockSpec((None, d, tf), lambda i, f: (layer, 0, f)),
                pl.BlockSpec((None, d, tf), lambda i, f: (layer, 0, f)),
                pl.BlockSpec((width, tf), lambda i, f: (0, f)),
                pl.BlockSpec((1, tf), lambda i, f: (0, f)),
                pl.BlockSpec((None, tf, d), lambda i, f: (layer, f, 0))]
    args = [x, g, w_up, w_gate, conv_w, conv_b, w_down]
    if final:
        in_specs.append(pl.BlockSpec((1, d), lambda i, f: (0, 0)))
        args.append(final_g)
    return pl.pallas_call(
        functools.partial(_ffn_kernel, tm=tm, tiles_per_seq=t_len // tm, final=final, n_chains=n_chains),
        grid=(n // tm, d_ff // tf),
        in_specs=in_specs,
        out_specs=pl.BlockSpec((tm, d), row),
        out_shape=jax.ShapeDtypeStruct((n, d), F32),
        scratch_shapes=[pltpu.VMEM((tm, d), BF16),
                        pltpu.VMEM((SUBLANES + tm, tf), F32),
                        pltpu.VMEM((d_ff // tf, SUBLANES, tf), F32)],
        compiler_params=_params("arbitrary", "arbitrary"),
    )(*args)


def kernel(x, mix_norm, ffn_norm, even_w_in, even_conv_w, even_conv_b, even_ln_g, even_ln_b, even_w_out,
           odd_w_qkv, odd_w_o, ffn_w_up, ffn_w_gate, ffn_conv_w, ffn_conv_b, ffn_w_down, final_norm):
    bsz, t_len, d = x.shape
    n = bsz * t_len
    depth = mix_norm.shape[0]
    xs = x.reshape(n, d)
    w_up, w_gate, w_down = ffn_w_up.astype(BF16), ffn_w_gate.astype(BF16), ffn_w_down.astype(BF16)
    for layer in range(depth):
        j = layer // 2
        g = mix_norm[layer][None, :]
        if layer % 2 == 0:
            w_in = even_w_in[j].astype(BF16)
            ag = _norm_matmul(xs, g, w_in, F32, 1024, 2048, 0, 2 * CONV_CH)
            qkv = _norm_matmul(xs, g, w_in, BF16, 1024, 1024, 2 * CONV_CH).reshape(bsz, t_len, -1)
            y_a = _conformer(ag.reshape(bsz, t_len, -1), even_conv_w[j], even_conv_b[j][None, :],
                             even_ln_g[j][None, :], even_ln_b[j][None, :], 256)
            y_b = _moba(qkv, MOBA_HEADS, MOBA_BLOCK, MOBA_TOPK, MOBA_KEY_TILE, MOBA_HEADS_PER_STEP)
            xs = _matmul_res([y_a.reshape(n, -1), y_b.reshape(n, -1)], even_w_out[j].astype(BF16), xs, 1024, 1024)
        else:
            qkv = _norm_matmul(xs, g, odd_w_qkv[j].astype(BF16), BF16, 1024, 2048)
            y = _stick_breaking(qkv.reshape(bsz, t_len, -1), SB_HEADS, SB_TILE, SB_HEADS_PER_STEP)
            xs = _matmul_res([y.reshape(n, -1)], odd_w_o[j].astype(BF16), xs, 1024, 1024)
        final_g = final_norm[None, :] if layer == depth - 1 else None
        xs = _ffn(xs, ffn_norm[layer][None, :], layer, w_up, w_gate, ffn_conv_w[layer], ffn_conv_b[layer][None, :],
                  w_down, final_g, t_len, 1024, 512, 2)
    return xs.reshape(bsz, t_len, d)
```

```python
import functools

import jax
import jax.numpy as jnp
from jax import lax
from jax.experimental import pallas as pl
from jax.experimental.pallas import tpu as pltpu

F32 = jnp.float32
BF16 = jnp.bfloat16

HEAD_DIM = 128
CONV_CH = 1024
CONV_WIDTH = 31
MOBA_HEADS = 8
MOBA_BLOCK = 256
MOBA_TOPK = 3
MOBA_KEY_TILE = 1024
MOBA_HEADS_PER_STEP = 4
SB_HEADS = 16
SB_TILE = 256
SB_HEADS_PER_STEP = 4
RMS_EPS = 1e-6
LN_EPS = 1e-5
LANES = 128
SUBLANES = 8
MASKED = -1e30
LOG2_E = 1.4426950408889634
SB_EXP_IS_ZERO = -105.0
VMEM_LIMIT = 58 * 1024 * 1024

_NT = (((1,), (1,)), ((), ()))


def _params(*semantics):
    return pltpu.CompilerParams(dimension_semantics=semantics, vmem_limit_bytes=VMEM_LIMIT)


def _rms(x, g):
    ms = jnp.mean(x * x, axis=-1, keepdims=True)
    return x * lax.rsqrt(ms + RMS_EPS) * g


def _cast_specs(casts, grid):
    n_inner = grid[1]
    in_specs, out_specs, out_shapes, operands = [], [], [], []
    for stack, layer, rb in casts:
        _, r, c = stack.shape
        assert r % rb == 0 and r // rb <= grid[0] * grid[1]
        last = r // rb - 1
        in_specs.append(pl.BlockSpec((None, rb, c), functools.partial(
            lambda i, j, layer, last: (layer, jnp.minimum(i * n_inner + j, last), 0), layer=layer, last=last)))
        out_specs.append(pl.BlockSpec((rb, c), functools.partial(
            lambda i, j, last: (jnp.minimum(i * n_inner + j, last), 0), last=last)))
        out_shapes.append(jax.ShapeDtypeStruct((r, c), BF16))
        operands.append(stack)
    return in_specs, out_specs, out_shapes, operands


def _run_casts(src_refs, dst_refs):
    for src_ref, dst_ref in zip(src_refs, dst_refs):
        dst_ref[...] = src_ref[...].astype(BF16)


def _norm_matmul_kernel(x_ref, g_ref, w_ref, o_ref, h_ref):
    @pl.when(pl.program_id(1) == 0)
    def _():
        h_ref[...] = _rms(x_ref[...], g_ref[...]).astype(BF16)

    o_ref[...] = jnp.dot(h_ref[...], w_ref[...], preferred_element_type=F32).astype(o_ref.dtype)


def _norm_matmul(x, g, w, out_dtype, tm, tn, col_start=0, col_stop=None):
    n, d = x.shape
    col_stop = w.shape[1] if col_stop is None else col_stop
    m = col_stop - col_start
    assert col_start % tn == 0 and m % tn == 0 and n % tm == 0
    first = col_start // tn
    return pl.pallas_call(
        _norm_matmul_kernel,
        grid=(n // tm, m // tn),
        in_specs=[pl.BlockSpec((tm, d), lambda i, j: (i, 0)),
                  pl.BlockSpec((1, d), lambda i, j: (0, 0)),
                  pl.BlockSpec((d, tn), lambda i, j: (0, first + j))],
        out_specs=pl.BlockSpec((tm, tn), lambda i, j: (i, j)),
        out_shape=jax.ShapeDtypeStruct((n, m), out_dtype),
        scratch_shapes=[pltpu.VMEM((tm, d), BF16)],
        compiler_params=_params("parallel", "arbitrary"),
    )(x, g, w)


def _matmul_res_kernel(*refs, n_pairs, n_casts):
    a_refs, w_refs = refs[:n_pairs], refs[n_pairs:2 * n_pairs]
    x_ref = refs[2 * n_pairs]
    cast_in = refs[2 * n_pairs + 1:2 * n_pairs + 1 + n_casts]
    o_ref = refs[2 * n_pairs + 1 + n_casts]
    acc = x_ref[...]
    for a_ref, w_ref in zip(a_refs, w_refs):
        acc = acc + jnp.dot(a_ref[...], w_ref[...], preferred_element_type=F32)
    o_ref[...] = acc
    _run_casts(cast_in, refs[2 * n_pairs + 2 + n_casts:])


def _matmul_res(a_list, w, x, tm, tn, casts=()):
    n, m = x.shape
    n_pairs = len(a_list)
    widths = [a.shape[1] for a in a_list]
    assert all(k == widths[0] for k in widths) and sum(widths) == w.shape[0]
    grid = (n // tm, m // tn)
    c_in, c_out, c_shapes, c_ops = _cast_specs(casts, grid)
    in_specs = ([pl.BlockSpec((tm, k), lambda i, j: (i, 0)) for k in widths]
                + [pl.BlockSpec((k, tn), functools.partial(lambda i, j, p: (p, j), p=p)) for p, k in enumerate(widths)]
                + [pl.BlockSpec((tm, tn), lambda i, j: (i, j))])
    return pl.pallas_call(
        functools.partial(_matmul_res_kernel, n_pairs=n_pairs, n_casts=len(casts)),
        grid=grid,
        in_specs=in_specs + c_in,
        out_specs=[pl.BlockSpec((tm, tn), lambda i, j: (i, j)), *c_out],
        out_shape=[jax.ShapeDtypeStruct((n, m), F32), *c_shapes],
        compiler_params=_params("arbitrary", "arbitrary"),
    )(*a_list, *([w] * n_pairs), x, *c_ops)


def _conformer_kernel(a_ref, g_ref, w_ref, b_ref, lng_ref, lnb_ref, o_ref, buf_ref, hs_ref, *, tt, halo):
    n_chunks = buf_ref.shape[0]
    width = w_ref.shape[1]

    @pl.when(pl.program_id(1) == 0)
    def _():
        buf_ref[:, 0:halo, :] = jnp.zeros((n_chunks, halo, LANES), F32)

    glu = a_ref[0] * jax.nn.sigmoid(g_ref[0])
    for c in range(n_chunks):
        buf_ref[c, halo:halo + tt, :] = glu[:, c * LANES:(c + 1) * LANES]

    def chunk_body(c, carry):
        acc = jnp.broadcast_to(b_ref[c], (tt, LANES))
        for k in range(width):
            acc = acc + buf_ref[c, pl.ds(halo - (width - 1) + k, tt), :] * w_ref[c, pl.ds(k, 1), :]
        hs_ref[c] = acc
        buf_ref[c, 0:halo, :] = buf_ref[c, tt:tt + halo, :]
        return carry

    lax.fori_loop(0, n_chunks, chunk_body, 0)

    channels = n_chunks * LANES
    total = hs_ref[0]
    for c in range(1, n_chunks):
        total = total + hs_ref[c]
    mu = jnp.sum(total, axis=-1, keepdims=True) * (1.0 / channels)
    sq = jnp.zeros((tt, LANES), F32)
    for c in range(n_chunks):
        d = hs_ref[c] - mu
        sq = sq + d * d
    var = jnp.sum(sq, axis=-1, keepdims=True) * (1.0 / channels)
    inv = lax.rsqrt(var + LN_EPS)
    for c in range(n_chunks):
        y = (hs_ref[c] - mu) * inv * lng_ref[c] + lnb_ref[c]
        o_ref[0, :, c * LANES:(c + 1) * LANES] = (y * jax.nn.sigmoid(y)).astype(o_ref.dtype)


def _conformer(ag, conv_w, conv_b, ln_g, ln_b, tt):
    bsz, t_len, c2 = ag.shape
    ch = c2 // 2
    n_chunks = ch // LANES
    width = conv_w.shape[0]
    halo = -(-(width - 1) // SUBLANES) * SUBLANES
    by_chunk = lambda p: p.reshape(-1, n_chunks, LANES).transpose(1, 0, 2)
    small = lambda rows: pl.BlockSpec((n_chunks, rows, LANES), lambda b, t: (0, 0, 0))
    return pl.pallas_call(
        functools.partial(_conformer_kernel, tt=tt, halo=halo),
        grid=(bsz, t_len // tt),
        in_specs=[pl.BlockSpec((1, tt, ch), lambda b, t: (b, t, 0)),
                  pl.BlockSpec((1, tt, ch), lambda b, t: (b, t, 1)),
                  small(width), small(1), small(1), small(1)],
        out_specs=pl.BlockSpec((1, tt, ch), lambda b, t: (b, t, 0)),
        out_shape=jax.ShapeDtypeStruct((bsz, t_len, ch), BF16),
        scratch_shapes=[pltpu.VMEM((n_chunks, halo + tt, LANES), F32),
                        pltpu.VMEM((n_chunks, tt, LANES), F32)],
        compiler_params=_params("parallel", "arbitrary"),
    )(ag, ag, by_chunk(conv_w), by_chunk(conv_b), by_chunk(ln_g), by_chunk(ln_b))


def _moba_kernel(q_ref, k_ref, v_ref, o_ref, kaug_ref, kmean_ref, *, blk, topk, key_tile):
    qi = pl.program_id(2)
    t_len = k_ref.shape[1]
    nb = t_len // blk
    heads = [slice(h * HEAD_DIM, (h + 1) * HEAD_DIM) for h in range(q_ref.shape[2] // HEAD_DIM)]
    c2 = HEAD_DIM ** -0.5 * LOG2_E

    @pl.when(qi == 0)
    def _():
        row = lax.broadcasted_iota(jnp.int32, (t_len, LANES), 0)
        col = lax.broadcasted_iota(jnp.int32, (t_len, LANES), 1)
        block_id = jnp.where(row // blk == col, 1.0, 0.0).astype(BF16)
        for h, hd in enumerate(heads):
            k = k_ref[0, :, hd]
            kaug_ref[h, :, 0:HEAD_DIM] = k
            kaug_ref[h, :, HEAD_DIM:HEAD_DIM + LANES] = block_id
            kmean = jnp.mean(k.astype(F32).reshape(nb, blk, HEAD_DIM), axis=1)
            kmean_ref[h] = kmean.astype(BF16)

    own = pl.ds(pl.multiple_of(qi * blk, blk), blk)
    r_i = lax.broadcasted_iota(jnp.int32, (blk, blk), 0)
    c_i = lax.broadcasted_iota(jnp.int32, (blk, blk), 1)
    causal = c_i <= r_i
    blk_id = lax.broadcasted_iota(jnp.int32, (nb, blk), 0)
    blk_id_f = blk_id.astype(F32)
    past = blk_id < qi

    def head_start(h):
        q = q_ref[0, :, heads[h]]
        gate = lax.dot_general(kmean_ref[h], q, _NT, preferred_element_type=F32)
        g = jnp.where(past, gate, -jnp.inf)
        allowed = jnp.zeros(gate.shape, jnp.bool_)
        for _ in range(topk):
            best = jnp.max(g, axis=0, keepdims=True)
            first = jnp.min(jnp.where(g == best, blk_id_f, float(nb)), axis=0, keepdims=True)
            pick = blk_id_f == first
            allowed = jnp.logical_or(allowed, jnp.logical_and(pick, past))
            g = jnp.where(pick, -jnp.inf, g)
        penalty_t = jnp.concatenate([jnp.where(allowed, 0.0, MASKED), jnp.zeros((LANES - nb, blk), F32)], axis=0)
        q_aug = jnp.concatenate([q, penalty_t.T.astype(BF16)], axis=1)
        s = lax.dot_general(q, k_ref[0, own, heads[h]], _NT, preferred_element_type=F32)
        s = jnp.where(causal, s, MASKED)
        m = jnp.max(s, axis=1, keepdims=True)
        p = jnp.exp2((s - m) * c2)
        l = jnp.sum(p, axis=1, keepdims=True)
        acc = jnp.dot(p.astype(BF16), v_ref[0, own, heads[h]], preferred_element_type=F32)
        return q_aug, (m, l, acc)

    started = [head_start(h) for h in range(len(heads))]
    q_augs = [qa for qa, _ in started]

    def body(j, carry):
        ks = pl.ds(pl.multiple_of(j * key_tile, key_tile), key_tile)
        out = []
        for h, (m, l, acc) in enumerate(carry):
            s = lax.dot_general(q_augs[h], kaug_ref[h, ks, :], _NT, preferred_element_type=F32)
            m_new = jnp.maximum(m, jnp.max(s, axis=1, keepdims=True))
            alpha = jnp.exp2((m - m_new) * c2)
            p = jnp.exp2((s - m_new) * c2)
            l = alpha * l + jnp.sum(p, axis=1, keepdims=True)
            acc = alpha * acc + jnp.dot(p.astype(BF16), v_ref[0, ks, heads[h]], preferred_element_type=F32)
            out.append((m_new, l, acc))
        return out

    n_tiles = (qi * blk + key_tile - 1) // key_tile
    final = lax.fori_loop(0, n_tiles, body, [st for _, st in started])
    for h, (m, l, acc) in enumerate(final):
        o_ref[0, :, heads[h]] = (acc / l).astype(o_ref.dtype)


def _moba(qkv, n_heads, blk, topk, key_tile, heads_per_step):
    bsz, t_len, _ = qkv.shape
    assert n_heads % heads_per_step == 0 and t_len % key_tile == 0 and key_tile % blk == 0
    assert (t_len // blk) % SUBLANES == 0 and t_len // blk <= LANES
    groups = n_heads // heads_per_step
    cols = heads_per_step * HEAD_DIM
    return pl.pallas_call(
        functools.partial(_moba_kernel, blk=blk, topk=topk, key_tile=key_tile),
        grid=(bsz, groups, t_len // blk),
        in_specs=[pl.BlockSpec((1, blk, cols), lambda b, h, i: (b, i, h)),
                  pl.BlockSpec((1, t_len, cols), lambda b, h, i: (b, 0, groups + h)),
                  pl.BlockSpec((1, t_len, cols), lambda b, h, i: (b, 0, 2 * groups + h))],
        out_specs=pl.BlockSpec((1, blk, cols), lambda b, h, i: (b, i, h)),
        out_shape=jax.ShapeDtypeStruct((bsz, t_len, n_heads * HEAD_DIM), BF16),
        scratch_shapes=[pltpu.VMEM((heads_per_step, t_len, HEAD_DIM + LANES), BF16),
                        pltpu.VMEM((heads_per_step, t_len // blk, HEAD_DIM), BF16)],
        compiler_params=_params("parallel", "parallel", "arbitrary"),
    )(qkv, qkv, qkv)


def _sb_kernel(q_ref, k_ref, v_ref, o_ref, *, tile):
    qi = pl.program_id(2)
    scale = HEAD_DIM ** -0.5
    heads = [slice(h * HEAD_DIM, (h + 1) * HEAD_DIM) for h in range(q_ref.shape[2] // HEAD_DIM)]
    neg_q = [-q_ref[0, :, hd] for hd in heads]
    r_i = lax.broadcasted_iota(jnp.int32, (tile, tile), 0)
    c_i = lax.broadcasted_iota(jnp.int32, (tile, tile), 1)
    later = jnp.where(r_i > c_i, 1.0, 0.0).astype(BF16)
    later2 = jnp.concatenate([later, later], axis=0)
    strict = c_i < r_i

    def tile_terms(h, j, diagonal):
        ks = pl.ds(pl.multiple_of(j * tile, tile), tile)
        nz = lax.dot_general(neg_q[h], k_ref[0, ks, heads[h]], _NT, preferred_element_type=F32) * scale
        log_1m = jnp.minimum(nz, 0.0) - jnp.log(1.0 + jnp.exp2(jnp.abs(nz) * -LOG2_E))
        if diagonal:
            log_1m = jnp.where(strict, log_1m, 0.0)
        hi = log_1m.astype(BF16)
        lo = (log_1m - hi.astype(F32)).astype(BF16)
        after = jnp.dot(jnp.concatenate([hi, lo], axis=1), later2, preferred_element_type=F32)
        return ks, log_1m - nz + after, jnp.sum(log_1m, axis=1, keepdims=True)

    def tile_update(h, terms, run, acc, diagonal):
        ks, log_w, total = terms
        a = jnp.exp(log_w + run)
        if diagonal:
            a = jnp.where(strict, a, 0.0)
        acc = acc + jnp.dot(a.astype(BF16), v_ref[0, ks, heads[h]], preferred_element_type=F32)
        return run + total, acc

    n_heads = len(heads)
    diag_terms = [tile_terms(h, qi, True) for h in range(n_heads)]
    prev_terms = [tile_terms(h, jnp.maximum(qi - 1, 0), False) for h in range(n_heads)]
    first_tile = jnp.where(qi > 0, 0.0, MASKED)
    runs, accs = [], []
    for h in range(n_heads):
        run, acc = tile_update(h, diag_terms[h], jnp.zeros((tile, 1), F32), jnp.zeros((tile, HEAD_DIM), F32), True)
        run, acc = tile_update(h, prev_terms[h], run + first_tile, acc, False)
        runs.append(run)
        accs.append(acc)

    def live(runs):
        return jnp.max(functools.reduce(jnp.maximum, runs)) > SB_EXP_IS_ZERO

    def cond(carry):
        j, go, _, _ = carry
        return jnp.logical_and(j >= 0, go)

    def body(carry):
        j, _, runs, accs = carry
        out = [tile_update(h, tile_terms(h, j, False), runs[h], accs[h], False) for h in range(n_heads)]
        runs, accs = [r for r, _ in out], [a for _, a in out]
        return j - 1, live(runs), runs, accs

    _, _, runs, accs = lax.while_loop(cond, body, (qi - 2, live(runs), runs, accs))
    for h in range(n_heads):
        o_ref[0, :, heads[h]] = accs[h].astype(o_ref.dtype)


def _stick_breaking(qkv, n_heads, tile, heads_per_step):
    bsz, t_len, _ = qkv.shape
    assert n_heads % heads_per_step == 0 and t_len % tile == 0
    groups = n_heads // heads_per_step
    cols = heads_per_step * HEAD_DIM
    return pl.pallas_call(
        functools.partial(_sb_kernel, tile=tile),
        grid=(bsz, groups, t_len // tile),
        in_specs=[pl.BlockSpec((1, tile, cols), lambda b, h, i: (b, i, h)),
                  pl.BlockSpec((1, t_len, cols), lambda b, h, i: (b, 0, groups + h)),
                  pl.BlockSpec((1, t_len, cols), lambda b, h, i: (b, 0, 2 * groups + h))],
        out_specs=pl.BlockSpec((1, tile, cols), lambda b, h, i: (b, i, h)),
        out_shape=jax.ShapeDtypeStruct((bsz, t_len, n_heads * HEAD_DIM), BF16),
        compiler_params=_params("parallel", "parallel", "arbitrary"),
    )(qkv, qkv, qkv)


def _ffn_kernel(*refs, tm, tiles_per_seq, final, n_chains):
    if final:
        x_ref, g_ref, wu_ref, wg_ref, cw_ref, cb_ref, wd_ref, fg_ref, o_ref, h_ref, ubuf_ref, halo_ref = refs
    else:
        x_ref, g_ref, wu_ref, wg_ref, cw_ref, cb_ref, wd_ref, o_ref, h_ref, ubuf_ref, halo_ref = refs
    i = pl.program_id(0)
    f = pl.program_id(1)
    width = cw_ref.shape[0]

    @pl.when(f == 0)
    def _():
        x = x_ref[...]
        h_ref[...] = _rms(x, g_ref[...]).astype(BF16)
        o_ref[...] = x

        @pl.when(i == 0)
        def _():
            halo_ref[...] = jnp.zeros(halo_ref.shape, F32)

    h = h_ref[...]
    seq_start = (i % tiles_per_seq) == 0
    tf = wu_ref.shape[1]
    chunk = tf // n_chains
    acts = []
    for c in range(n_chains):
        cols = slice(c * chunk, (c + 1) * chunk)
        up = jnp.dot(h, wu_ref[:, cols], preferred_element_type=F32)
        gate = jnp.dot(h, wg_ref[:, cols], preferred_element_type=F32)
        ubuf_ref[0:SUBLANES, cols] = jnp.where(seq_start, 0.0, halo_ref[f, :, cols])
        ubuf_ref[SUBLANES:SUBLANES + tm, cols] = up
        halo_ref[f, :, cols] = up[tm - SUBLANES:tm, :]
        conv = up * cw_ref[width - 1:width, cols] + cb_ref[:, cols]
        for k in range(width - 1):
            conv = conv + ubuf_ref[pl.ds(SUBLANES - (width - 1) + k, tm), cols] * cw_ref[k:k + 1, cols]
        acts.append((conv * jax.nn.sigmoid(conv) * gate).astype(BF16))
    o_ref[...] += jnp.dot(jnp.concatenate(acts, axis=1), wd_ref[...], preferred_element_type=F32)

    if final:
        @pl.when(f == pl.num_programs(1) - 1)
        def _():
            o_ref[...] = _rms(o_ref[...], fg_ref[...])


def _ffn(x, g, w_up, w_gate, conv_w, conv_b, w_down, final_g, t_len, tm, tf, n_chains):
    n, d = x.shape
    d_ff = w_up.shape[1]
    width = conv_w.shape[0]
    final = final_g is not None
    row = lambda i, f: (i, 0)
    in_specs = [pl.BlockSpec((tm, d), row),
                pl.BlockSpec((1, d), lambda i, f: (0, 0)),
                pl.BlockSpec((d, tf), lambda i, f: (0, f)),
                pl.BlockSpec((d, tf), lambda i, f: (0, f)),
                pl.BlockSpec((width, tf), lambda i, f: (0, f)),
                pl.BlockSpec((1, tf), lambda i, f: (0, f)),
                pl.BlockSpec((tf, d), lambda i, f: (f, 0))]
    args = [x, g, w_up, w_gate, conv_w, conv_b, w_down]
    if final:
        in_specs.append(pl.BlockSpec((1, d), lambda i, f: (0, 0)))
        args.append(final_g)
    return pl.pallas_call(
        functools.partial(_ffn_kernel, tm=tm, tiles_per_seq=t_len // tm, final=final, n_chains=n_chains),
        grid=(n // tm, d_ff // tf),
        in_specs=in_specs,
        out_specs=pl.BlockSpec((tm, d), row),
        out_shape=jax.ShapeDtypeStruct((n, d), F32),
        scratch_shapes=[pltpu.VMEM((tm, d), BF16),
                        pltpu.VMEM((SUBLANES + tm, tf), F32),
                        pltpu.VMEM((d_ff // tf, SUBLANES, tf), F32)],
        compiler_params=_params("arbitrary", "arbitrary"),
    )(*args)


def kernel(x, mix_norm, ffn_norm, even_w_in, even_conv_w, even_conv_b, even_ln_g, even_ln_b, even_w_out,
           odd_w_qkv, odd_w_o, ffn_w_up, ffn_w_gate, ffn_conv_w, ffn_conv_b, ffn_w_down, final_norm):
    bsz, t_len, d = x.shape
    n = bsz * t_len
    depth = mix_norm.shape[0]
    xs = x.reshape(n, d)
    out_tm = out_tn = 1024
    out_steps = (n // out_tm) * (d // out_tn)

    def ffn_casts(layer):
        def rows(r):
            return next(rb for rb in range(2 * SUBLANES, r + 1, 2 * SUBLANES) if r % rb == 0 and r // rb <= out_steps)
        return [(w, layer, rows(w.shape[1])) for w in (ffn_w_up, ffn_w_gate, ffn_w_down)]

    for layer in range(depth):
        j = layer // 2
        g = mix_norm[layer][None, :]
        if layer % 2 == 0:
            w_in = even_w_in[j].astype(BF16)
            ag = _norm_matmul(xs, g, w_in, F32, 1024, 2048, 0, 2 * CONV_CH)
            qkv = _norm_matmul(xs, g, w_in, BF16, 1024, 1024, 2 * CONV_CH).reshape(bsz, t_len, -1)
            y_a = _conformer(ag.reshape(bsz, t_len, -1), even_conv_w[j], even_conv_b[j][None, :],
                             even_ln_g[j][None, :], even_ln_b[j][None, :], 256)
            y_b = _moba(qkv, MOBA_HEADS, MOBA_BLOCK, MOBA_TOPK, MOBA_KEY_TILE, MOBA_HEADS_PER_STEP)
            mixed, w_proj = [y_a.reshape(n, -1), y_b.reshape(n, -1)], even_w_out[j]
        else:
            qkv = _norm_matmul(xs, g, odd_w_qkv[j].astype(BF16), BF16, 1024, 2048)
            y = _stick_breaking(qkv.reshape(bsz, t_len, -1), SB_HEADS, SB_TILE, SB_HEADS_PER_STEP)
            mixed, w_proj = [y.reshape(n, -1)], odd_w_o[j]
        xs, w_up, w_gate, w_down = _matmul_res(mixed, w_proj.astype(BF16), xs, out_tm, out_tn, ffn_casts(layer))
        final_g = final_norm[None, :] if layer == depth - 1 else None
        xs = _ffn(xs, ffn_norm[layer][None, :], w_up, w_gate, ffn_conv_w[layer], ffn_conv_b[layer][None, :],
                  w_down, final_g, t_len, 1024, 512, 2)
    return xs.reshape(bsz, t_len, d)
```

```python
import functools

import jax
import jax.numpy as jnp
from jax import lax
from jax.experimental import pallas as pl
from jax.experimental.pallas import tpu as pltpu

F32 = jnp.float32
BF16 = jnp.bfloat16

HEAD_DIM = 128
CONV_CH = 1024
CONV_WIDTH = 31
MOBA_HEADS = 8
MOBA_BLOCK = 256
MOBA_TOPK = 3
MOBA_KEY_TILE = 1024
MOBA_HEADS_PER_STEP = 4
SB_HEADS = 16
SB_TILE = 256
SB_HEADS_PER_STEP = 4
RMS_EPS = 1e-6
LN_EPS = 1e-5
LANES = 128
SUBLANES = 8
MASKED = -1e30
LOG2_E = 1.4426950408889634
SB_EXP_IS_ZERO = -105.0
VMEM_LIMIT = 58 * 1024 * 1024

_NT = (((1,), (1,)), ((), ()))


def _params(*semantics):
    return pltpu.CompilerParams(dimension_semantics=semantics, vmem_limit_bytes=VMEM_LIMIT)


def _rms(x, g):
    ms = jnp.mean(x * x, axis=-1, keepdims=True)
    return x * lax.rsqrt(ms + RMS_EPS) * g


def _cast_specs(casts, grid):
    def step(idx):
        s = idx[0]
        for extent, i in zip(grid[1:], idx[1:]):
            s = s * extent + i
        return s

    n_steps = functools.reduce(lambda a, b: a * b, grid)
    in_specs, out_specs, out_shapes, operands = [], [], [], []
    for stack, layer, rb in casts:
        _, r, c = stack.shape
        assert r % rb == 0 and r // rb <= n_steps
        last = r // rb - 1
        in_specs.append(pl.BlockSpec((None, rb, c), functools.partial(
            lambda *idx, layer, last: (layer, jnp.minimum(step(idx), last), 0), layer=layer, last=last)))
        out_specs.append(pl.BlockSpec((rb, c), functools.partial(
            lambda *idx, last: (jnp.minimum(step(idx), last), 0), last=last)))
        out_shapes.append(jax.ShapeDtypeStruct((r, c), BF16))
        operands.append(stack)
    return in_specs, out_specs, out_shapes, operands


def _run_casts(src_refs, dst_refs):
    for src_ref, dst_ref in zip(src_refs, dst_refs):
        dst_ref[...] = src_ref[...].astype(BF16)


def _norm_matmul_kernel(x_ref, g_ref, w_ref, o_ref, h_ref):
    @pl.when(pl.program_id(1) == 0)
    def _():
        h_ref[...] = _rms(x_ref[...], g_ref[...]).astype(BF16)

    o_ref[...] = jnp.dot(h_ref[...], w_ref[...], preferred_element_type=F32).astype(o_ref.dtype)


def _norm_matmul(x, g, w, out_dtype, tm, tn, col_start=0, col_stop=None):
    n, d = x.shape
    col_stop = w.shape[1] if col_stop is None else col_stop
    m = col_stop - col_start
    assert col_start % tn == 0 and m % tn == 0 and n % tm == 0
    first = col_start // tn
    return pl.pallas_call(
        _norm_matmul_kernel,
        grid=(n // tm, m // tn),
        in_specs=[pl.BlockSpec((tm, d), lambda i, j: (i, 0)),
                  pl.BlockSpec((1, d), lambda i, j: (0, 0)),
                  pl.BlockSpec((d, tn), lambda i, j: (0, first + j))],
        out_specs=pl.BlockSpec((tm, tn), lambda i, j: (i, j)),
        out_shape=jax.ShapeDtypeStruct((n, m), out_dtype),
        scratch_shapes=[pltpu.VMEM((tm, d), BF16)],
        compiler_params=_params("parallel", "arbitrary"),
    )(x, g, w)


def _matmul_res_kernel(*refs, n_pairs):
    a_refs, w_refs = refs[:n_pairs], refs[n_pairs:2 * n_pairs]
    x_ref, o_ref = refs[2 * n_pairs], refs[2 * n_pairs + 1]
    acc = x_ref[...]
    for a_ref, w_ref in zip(a_refs, w_refs):
        acc = acc + jnp.dot(a_ref[...], w_ref[...], preferred_element_type=F32)
    o_ref[...] = acc


def _matmul_res(a_list, w, x, tm, tn):
    n, m = x.shape
    n_pairs = len(a_list)
    widths = [a.shape[1] for a in a_list]
    assert all(k == widths[0] for k in widths) and sum(widths) == w.shape[0]
    in_specs = ([pl.BlockSpec((tm, k), lambda i, j: (i, 0)) for k in widths]
                + [pl.BlockSpec((k, tn), functools.partial(lambda i, j, p: (p, j), p=p)) for p, k in enumerate(widths)]
                + [pl.BlockSpec((tm, tn), lambda i, j: (i, j))])
    return pl.pallas_call(
        functools.partial(_matmul_res_kernel, n_pairs=n_pairs),
        grid=(n // tm, m // tn),
        in_specs=in_specs,
        out_specs=pl.BlockSpec((tm, tn), lambda i, j: (i, j)),
        out_shape=jax.ShapeDtypeStruct((n, m), F32),
        compiler_params=_params("parallel", "arbitrary"),
    )(*a_list, *([w] * n_pairs), x)


def _conformer_kernel(a_ref, g_ref, w_ref, b_ref, lng_ref, lnb_ref, o_ref, buf_ref, hs_ref, *, tt, halo):
    n_chunks = buf_ref.shape[0]
    width = w_ref.shape[1]

    @pl.when(pl.program_id(1) == 0)
    def _():
        buf_ref[:, 0:halo, :] = jnp.zeros((n_chunks, halo, LANES), F32)

    glu = a_ref[0] * jax.nn.sigmoid(g_ref[0])
    for c in range(n_chunks):
        buf_ref[c, halo:halo + tt, :] = glu[:, c * LANES:(c + 1) * LANES]

    def chunk_body(c, carry):
        acc = jnp.broadcast_to(b_ref[c], (tt, LANES))
        for k in range(width):
            acc = acc + buf_ref[c, pl.ds(halo - (width - 1) + k, tt), :] * w_ref[c, pl.ds(k, 1), :]
        hs_ref[c] = acc
        buf_ref[c, 0:halo, :] = buf_ref[c, tt:tt + halo, :]
        return carry

    lax.fori_loop(0, n_chunks, chunk_body, 0)

    channels = n_chunks * LANES
    total = hs_ref[0]
    for c in range(1, n_chunks):
        total = total + hs_ref[c]
    mu = jnp.sum(total, axis=-1, keepdims=True) * (1.0 / channels)
    sq = jnp.zeros((tt, LANES), F32)
    for c in range(n_chunks):
        d = hs_ref[c] - mu
        sq = sq + d * d
    var = jnp.sum(sq, axis=-1, keepdims=True) * (1.0 / channels)
    inv = lax.rsqrt(var + LN_EPS)
    for c in range(n_chunks):
        y = (hs_ref[c] - mu) * inv * lng_ref[c] + lnb_ref[c]
        o_ref[0, :, c * LANES:(c + 1) * LANES] = (y * jax.nn.sigmoid(y)).astype(o_ref.dtype)


def _conformer(ag, conv_w, conv_b, ln_g, ln_b, tt):
    bsz, t_len, c2 = ag.shape
    ch = c2 // 2
    n_chunks = ch // LANES
    width = conv_w.shape[0]
    halo = -(-(width - 1) // SUBLANES) * SUBLANES
    by_chunk = lambda p: p.reshape(-1, n_chunks, LANES).transpose(1, 0, 2)
    small = lambda rows: pl.BlockSpec((n_chunks, rows, LANES), lambda b, t: (0, 0, 0))
    return pl.pallas_call(
        functools.partial(_conformer_kernel, tt=tt, halo=halo),
        grid=(bsz, t_len // tt),
        in_specs=[pl.BlockSpec((1, tt, ch), lambda b, t: (b, t, 0)),
                  pl.BlockSpec((1, tt, ch), lambda b, t: (b, t, 1)),
                  small(width), small(1), small(1), small(1)],
        out_specs=pl.BlockSpec((1, tt, ch), lambda b, t: (b, t, 0)),
        out_shape=jax.ShapeDtypeStruct((bsz, t_len, ch), BF16),
        scratch_shapes=[pltpu.VMEM((n_chunks, halo + tt, LANES), F32),
                        pltpu.VMEM((n_chunks, tt, LANES), F32)],
        compiler_params=_params("parallel", "arbitrary"),
    )(ag, ag, by_chunk(conv_w), by_chunk(conv_b), by_chunk(ln_g), by_chunk(ln_b))


def _moba_kernel(*refs, blk, topk, key_tile, n_casts):
    q_ref, k_ref, v_ref = refs[:3]
    o_ref = refs[3 + n_casts]
    kaug_ref, kmean_ref = refs[-2:]
    _run_casts(refs[3:3 + n_casts], refs[4 + n_casts:4 + 2 * n_casts])
    qi = pl.program_id(2)
    t_len = k_ref.shape[1]
    nb = t_len // blk
    heads = [slice(h * HEAD_DIM, (h + 1) * HEAD_DIM) for h in range(q_ref.shape[2] // HEAD_DIM)]
    c2 = HEAD_DIM ** -0.5 * LOG2_E

    @pl.when(qi == 0)
    def _():
        row = lax.broadcasted_iota(jnp.int32, (t_len, LANES), 0)
        col = lax.broadcasted_iota(jnp.int32, (t_len, LANES), 1)
        block_id = jnp.where(row // blk == col, 1.0, 0.0).astype(BF16)
        for h, hd in enumerate(heads):
            k = k_ref[0, :, hd]
            kaug_ref[h, :, 0:HEAD_DIM] = k
            kaug_ref[h, :, HEAD_DIM:HEAD_DIM + LANES] = block_id
            kmean = jnp.mean(k.astype(F32).reshape(nb, blk, HEAD_DIM), axis=1)
            kmean_ref[h] = kmean.astype(BF16)

    own = pl.ds(pl.multiple_of(qi * blk, blk), blk)
    r_i = lax.broadcasted_iota(jnp.int32, (blk, blk), 0)
    c_i = lax.broadcasted_iota(jnp.int32, (blk, blk), 1)
    causal = c_i <= r_i
    blk_id = lax.broadcasted_iota(jnp.int32, (nb, blk), 0)
    blk_id_f = blk_id.astype(F32)
    past = blk_id < qi

    def head_start(h):
        q = q_ref[0, :, heads[h]]
        gate = lax.dot_general(kmean_ref[h], q, _NT, preferred_element_type=F32)
        g = jnp.where(past, gate, -jnp.inf)
        allowed = jnp.zeros(gate.shape, jnp.bool_)
        for _ in range(topk):
            best = jnp.max(g, axis=0, keepdims=True)
            first = jnp.min(jnp.where(g == best, blk_id_f, float(nb)), axis=0, keepdims=True)
            pick = blk_id_f == first
            allowed = jnp.logical_or(allowed, jnp.logical_and(pick, past))
            g = jnp.where(pick, -jnp.inf, g)
        penalty_t = jnp.concatenate([jnp.where(allowed, 0.0, MASKED), jnp.zeros((LANES - nb, blk), F32)], axis=0)
        q_aug = jnp.concatenate([q, penalty_t.T.astype(BF16)], axis=1)
        s = lax.dot_general(q, k_ref[0, own, heads[h]], _NT, preferred_element_type=F32)
        s = jnp.where(causal, s, MASKED)
        m = jnp.max(s, axis=1, keepdims=True)
        p = jnp.exp2((s - m) * c2)
        l = jnp.sum(p, axis=1, keepdims=True)
        acc = jnp.dot(p.astype(BF16), v_ref[0, own, heads[h]], preferred_element_type=F32)
        return q_aug, (m, l, acc)

    started = [head_start(h) for h in range(len(heads))]
    q_augs = [qa for qa, _ in started]

    def body(j, carry):
        ks = pl.ds(pl.multiple_of(j * key_tile, key_tile), key_tile)
        out = []
        for h, (m, l, acc) in enumerate(carry):
            s = lax.dot_general(q_augs[h], kaug_ref[h, ks, :], _NT, preferred_element_type=F32)
            m_new = jnp.maximum(m, jnp.max(s, axis=1, keepdims=True))
            alpha = jnp.exp2((m - m_new) * c2)
            p = jnp.exp2((s - m_new) * c2)
            l = alpha * l + jnp.sum(p, axis=1, keepdims=True)
            acc = alpha * acc + jnp.dot(p.astype(BF16), v_ref[0, ks, heads[h]], preferred_element_type=F32)
            out.append((m_new, l, acc))
        return out

    n_tiles = (qi * blk + key_tile - 1) // key_tile
    final = lax.fori_loop(0, n_tiles, body, [st for _, st in started])
    for h, (m, l, acc) in enumerate(final):
        o_ref[0, :, heads[h]] = (acc / l).astype(o_ref.dtype)


def _attention_grid(qkv, n_heads, tile, heads_per_step):
    bsz, t_len, _ = qkv.shape
    assert n_heads % heads_per_step == 0 and t_len % tile == 0
    groups = n_heads // heads_per_step
    cols = heads_per_step * HEAD_DIM
    q_spec = pl.BlockSpec((1, tile, cols), lambda b, h, i: (b, i, h))
    k_spec = pl.BlockSpec((1, t_len, cols), lambda b, h, i: (b, 0, groups + h))
    v_spec = pl.BlockSpec((1, t_len, cols), lambda b, h, i: (b, 0, 2 * groups + h))
    out_shape = jax.ShapeDtypeStruct((bsz, t_len, n_heads * HEAD_DIM), BF16)
    return (bsz, groups, t_len // tile), [q_spec, k_spec, v_spec], q_spec, out_shape


def _moba(qkv, n_heads, blk, topk, key_tile, heads_per_step, casts):
    t_len = qkv.shape[1]
    assert t_len % key_tile == 0 and key_tile % blk == 0
    assert (t_len // blk) % SUBLANES == 0 and t_len // blk <= LANES
    grid, in_specs, out_spec, out_shape = _attention_grid(qkv, n_heads, blk, heads_per_step)
    c_in, c_out, c_shapes, c_ops = _cast_specs(casts, grid)
    return pl.pallas_call(
        functools.partial(_moba_kernel, blk=blk, topk=topk, key_tile=key_tile, n_casts=len(casts)),
        grid=grid,
        in_specs=in_specs + c_in,
        out_specs=[out_spec, *c_out],
        out_shape=[out_shape, *c_shapes],
        scratch_shapes=[pltpu.VMEM((heads_per_step, t_len, HEAD_DIM + LANES), BF16),
                        pltpu.VMEM((heads_per_step, t_len // blk, HEAD_DIM), BF16)],
        compiler_params=_params("arbitrary", "arbitrary", "arbitrary"),
    )(qkv, qkv, qkv, *c_ops)


def _sb_kernel(*refs, tile, n_casts):
    q_ref, k_ref, v_ref = refs[:3]
    o_ref = refs[3 + n_casts]
    _run_casts(refs[3:3 + n_casts], refs[4 + n_casts:])
    qi = pl.program_id(2)
    scale = HEAD_DIM ** -0.5
    heads = [slice(h * HEAD_DIM, (h + 1) * HEAD_DIM) for h in range(q_ref.shape[2] // HEAD_DIM)]
    neg_q = [-q_ref[0, :, hd] for hd in heads]
    r_i = lax.broadcasted_iota(jnp.int32, (tile, tile), 0)
    c_i = lax.broadcasted_iota(jnp.int32, (tile, tile), 1)
    later = jnp.where(r_i > c_i, 1.0, 0.0).astype(BF16)
    later2 = jnp.concatenate([later, later], axis=0)
    strict = c_i < r_i

    def tile_terms(h, j, diagonal):
        ks = pl.ds(pl.multiple_of(j * tile, tile), tile)
        nz = lax.dot_general(neg_q[h], k_ref[0, ks, heads[h]], _NT, preferred_element_type=F32) * scale
        log_1m = jnp.minimum(nz, 0.0) - jnp.log(1.0 + jnp.exp2(jnp.abs(nz) * -LOG2_E))
        if diagonal:
            log_1m = jnp.where(strict, log_1m, 0.0)
        hi = log_1m.astype(BF16)
        lo = (log_1m - hi.astype(F32)).astype(BF16)
        after = jnp.dot(jnp.concatenate([hi, lo], axis=1), later2, preferred_element_type=F32)
        return ks, log_1m - nz + after, jnp.sum(log_1m, axis=1, keepdims=True)

    def tile_update(h, terms, run, acc, diagonal):
        ks, log_w, total = terms
        a = jnp.exp(log_w + run)
        if diagonal:
            a = jnp.where(strict, a, 0.0)
        acc = acc + jnp.dot(a.astype(BF16), v_ref[0, ks, heads[h]], preferred_element_type=F32)
        return run + total, acc

    n_heads = len(heads)
    diag_terms = [tile_terms(h, qi, True) for h in range(n_heads)]
    prev_terms = [tile_terms(h, jnp.maximum(qi - 1, 0), False) for h in range(n_heads)]
    first_tile = jnp.where(qi > 0, 0.0, MASKED)
    runs, accs = [], []
    for h in range(n_heads):
        run, acc = tile_update(h, diag_terms[h], jnp.zeros((tile, 1), F32), jnp.zeros((tile, HEAD_DIM), F32), True)
        run, acc = tile_update(h, prev_terms[h], run + first_tile, acc, False)
        runs.append(run)
        accs.append(acc)

    def live(runs):
        return jnp.max(functools.reduce(jnp.maximum, runs)) > SB_EXP_IS_ZERO

    def cond(carry):
        j, go, _, _ = carry
        return jnp.logical_and(j >= 0, go)

    def body(carry):
        j, _, runs, accs = carry
        out = [tile_update(h, tile_terms(h, j, False), runs[h], accs[h], False) for h in range(n_heads)]
        runs, accs = [r for r, _ in out], [a for _, a in out]
        return j - 1, live(runs), runs, accs

    _, _, runs, accs = lax.while_loop(cond, body, (qi - 2, live(runs), runs, accs))
    for h in range(n_heads):
        o_ref[0, :, heads[h]] = accs[h].astype(o_ref.dtype)


def _stick_breaking(qkv, n_heads, tile, heads_per_step, casts):
    grid, in_specs, out_spec, out_shape = _attention_grid(qkv, n_heads, tile, heads_per_step)
    c_in, c_out, c_shapes, c_ops = _cast_specs(casts, grid)
    return pl.pallas_call(
        functools.partial(_sb_kernel, tile=tile, n_casts=len(casts)),
        grid=grid,
        in_specs=in_specs + c_in,
        out_specs=[out_spec, *c_out],
        out_shape=[out_shape, *c_shapes],
        compiler_params=_params("arbitrary", "arbitrary", "arbitrary"),
    )(qkv, qkv, qkv, *c_ops)


def _ffn_kernel(*refs, tm, tiles_per_seq, final, n_chains):
    if final:
        x_ref, g_ref, wu_ref, wg_ref, cw_ref, cb_ref, wd_ref, fg_ref, o_ref, h_ref, ubuf_ref, halo_ref = refs
    else:
        x_ref, g_ref, wu_ref, wg_ref, cw_ref, cb_ref, wd_ref, o_ref, h_ref, ubuf_ref, halo_ref = refs
    i = pl.program_id(0)
    f = pl.program_id(1)
    width = cw_ref.shape[0]

    @pl.when(f == 0)
    def _():
        x = x_ref[...]
        h_ref[...] = _rms(x, g_ref[...]).astype(BF16)
        o_ref[...] = x

        @pl.when(i == 0)
        def _():
            halo_ref[...] = jnp.zeros(halo_ref.shape, F32)

    h = h_ref[...]
    seq_start = (i % tiles_per_seq) == 0
    tf = wu_ref.shape[1]
    chunk = tf // n_chains
    acts = []
    for c in range(n_chains):
        cols = slice(c * chunk, (c + 1) * chunk)
        up = jnp.dot(h, wu_ref[:, cols], preferred_element_type=F32)
        gate = jnp.dot(h, wg_ref[:, cols], preferred_element_type=F32)
        ubuf_ref[0:SUBLANES, cols] = jnp.where(seq_start, 0.0, halo_ref[f, :, cols])
        ubuf_ref[SUBLANES:SUBLANES + tm, cols] = up
        halo_ref[f, :, cols] = up[tm - SUBLANES:tm, :]
        conv = up * cw_ref[width - 1:width, cols] + cb_ref[:, cols]
        for k in range(width - 1):
            conv = conv + ubuf_ref[pl.ds(SUBLANES - (width - 1) + k, tm), cols] * cw_ref[k:k + 1, cols]
        acts.append((conv * jax.nn.sigmoid(conv) * gate).astype(BF16))
    o_ref[...] += jnp.dot(jnp.concatenate(acts, axis=1), wd_ref[...], preferred_element_type=F32)

    if final:
        @pl.when(f == pl.num_programs(1) - 1)
        def _():
            o_ref[...] = _rms(o_ref[...], fg_ref[...])


def _ffn(x, g, w_up, w_gate, conv_w, conv_b, w_down, final_g, t_len, tm, tf, n_chains):
    n, d = x.shape
    d_ff = w_up.shape[1]
    width = conv_w.shape[0]
    final = final_g is not None
    row = lambda i, f: (i, 0)
    in_specs = [pl.BlockSpec((tm, d), row),
                pl.BlockSpec((1, d), lambda i, f: (0, 0)),
                pl.BlockSpec((d, tf), lambda i, f: (0, f)),
                pl.BlockSpec((d, tf), lambda i, f: (0, f)),
                pl.BlockSpec((width, tf), lambda i, f: (0, f)),
                pl.BlockSpec((1, tf), lambda i, f: (0, f)),
                pl.BlockSpec((tf, d), lambda i, f: (f, 0))]
    args = [x, g, w_up, w_gate, conv_w, conv_b, w_down]
    if final:
        in_specs.append(pl.BlockSpec((1, d), lambda i, f: (0, 0)))
        args.append(final_g)
    return pl.pallas_call(
        functools.partial(_ffn_kernel, tm=tm, tiles_per_seq=t_len // tm, final=final, n_chains=n_chains),
        grid=(n // tm, d_ff // tf),
        in_specs=in_specs,
        out_specs=pl.BlockSpec((tm, d), row),
        out_shape=jax.ShapeDtypeStruct((n, d), F32),
        scratch_shapes=[pltpu.VMEM((tm, d), BF16),
                        pltpu.VMEM((SUBLANES + tm, tf), F32),
                        pltpu.VMEM((d_ff // tf, SUBLANES, tf), F32)],
        compiler_params=_params("arbitrary", "arbitrary"),
    )(*args)


def kernel(x, mix_norm, ffn_norm, even_w_in, even_conv_w, even_conv_b, even_ln_g, even_ln_b, even_w_out,
           odd_w_qkv, odd_w_o, ffn_w_up, ffn_w_gate, ffn_conv_w, ffn_conv_b, ffn_w_down, final_norm):
    bsz, t_len, d = x.shape
    n = bsz * t_len
    depth = mix_norm.shape[0]
    xs = x.reshape(n, d)

    def ffn_casts(layer, steps):
        def rows(r):
            return next(rb for rb in range(2 * SUBLANES, r + 1, 2 * SUBLANES) if r % rb == 0 and r // rb <= steps)
        return [(w, layer, rows(w.shape[1])) for w in (ffn_w_up, ffn_w_gate, ffn_w_down)]

    for layer in range(depth):
        j = layer // 2
        g = mix_norm[layer][None, :]
        if layer % 2 == 0:
            w_in = even_w_in[j].astype(BF16)
            ag = _norm_matmul(xs, g, w_in, F32, 1024, 2048, 0, 2 * CONV_CH)
            qkv = _norm_matmul(xs, g, w_in, BF16, 1024, 1024, 2 * CONV_CH).reshape(bsz, t_len, -1)
            y_a = _conformer(ag.reshape(bsz, t_len, -1), even_conv_w[j], even_conv_b[j][None, :],
                             even_ln_g[j][None, :], even_ln_b[j][None, :], 256)
            steps = bsz * (MOBA_HEADS // MOBA_HEADS_PER_STEP) * (t_len // MOBA_BLOCK)
            y_b, w_up, w_gate, w_down = _moba(qkv, MOBA_HEADS, MOBA_BLOCK, MOBA_TOPK, MOBA_KEY_TILE,
                                              MOBA_HEADS_PER_STEP, ffn_casts(layer, steps))
            xs = _matmul_res([y_a.reshape(n, -1), y_b.reshape(n, -1)], even_w_out[j].astype(BF16), xs, 1024, 1024)
        else:
            qkv = _norm_matmul(xs, g, odd_w_qkv[j].astype(BF16), BF16, 1024, 2048)
            steps = bsz * (SB_HEADS // SB_HEADS_PER_STEP) * (t_len // SB_TILE)
            y, w_up, w_gate, w_down = _stick_breaking(qkv.reshape(bsz, t_len, -1), SB_HEADS, SB_TILE,
                                                      SB_HEADS_PER_STEP, ffn_casts(layer, steps))
            xs = _matmul_res([y.reshape(n, -1)], odd_w_o[j].astype(BF16), xs, 1024, 1024)
        final_g = final_norm[None, :] if layer == depth - 1 else None
        xs = _ffn(xs, ffn_norm[layer][None, :], w_up, w_gate, ffn_conv_w[layer], ffn_conv_b[layer][None, :],
                  w_down, final_g, t_len, 1024, 512, 2)
    return xs.reshape(bsz, t_len, d)
```

```python
import functools

import jax
import jax.numpy as jnp
from jax import lax
from jax.experimental import pallas as pl
from jax.experimental.pallas import tpu as pltpu

F32 = jnp.float32
BF16 = jnp.bfloat16

HEAD_DIM = 128
CONV_CH = 1024
CONV_WIDTH = 31
MOBA_HEADS = 8
MOBA_BLOCK = 256
MOBA_TOPK = 3
MOBA_KEY_TILE = 1024
MOBA_HEADS_PER_STEP = 4
SB_HEADS = 16
SB_TILE = 256
SB_HEADS_PER_STEP = 4
RMS_EPS = 1e-6
LN_EPS = 1e-5
LANES = 128
SUBLANES = 8
MASKED = -1e30
LOG2_E = 1.4426950408889634
SB_EXP_IS_ZERO = -105.0
VMEM_LIMIT = 58 * 1024 * 1024

_NT = (((1,), (1,)), ((), ()))


def _params(*semantics):
    return pltpu.CompilerParams(dimension_semantics=semantics, vmem_limit_bytes=VMEM_LIMIT)


def _rms(x, g):
    ms = jnp.mean(x * x, axis=-1, keepdims=True)
    return x * lax.rsqrt(ms + RMS_EPS) * g


def _cast_specs(casts, grid):
    def step(idx):
        s = idx[0]
        for extent, i in zip(grid[1:], idx[1:]):
            s = s * extent + i
        return s

    n_steps = functools.reduce(lambda a, b: a * b, grid)
    in_specs, out_specs, out_shapes, operands = [], [], [], []
    for stack, layer, rb in casts:
        _, r, c = stack.shape
        assert r % rb == 0 and r // rb <= n_steps
        last = r // rb - 1
        in_specs.append(pl.BlockSpec((None, rb, c), functools.partial(
            lambda *idx, layer, last: (layer, jnp.minimum(step(idx), last), 0), layer=layer, last=last)))
        out_specs.append(pl.BlockSpec((rb, c), functools.partial(
            lambda *idx, last: (jnp.minimum(step(idx), last), 0), last=last)))
        out_shapes.append(jax.ShapeDtypeStruct((r, c), BF16))
        operands.append(stack)
    return in_specs, out_specs, out_shapes, operands


def _run_casts(src_refs, dst_refs):
    for src_ref, dst_ref in zip(src_refs, dst_refs):
        dst_ref[...] = src_ref[...].astype(BF16)


def _norm_matmul_kernel(x_ref, g_ref, w_ref, o_ref, h_ref):
    @pl.when(pl.program_id(1) == 0)
    def _():
        h_ref[...] = _rms(x_ref[...], g_ref[...]).astype(BF16)

    o_ref[...] = jnp.dot(h_ref[...], w_ref[...], preferred_element_type=F32).astype(o_ref.dtype)


def _norm_matmul(x, g, w, out_dtype, tm, tn, col_start=0, col_stop=None):
    n, d = x.shape
    col_stop = w.shape[1] if col_stop is None else col_stop
    m = col_stop - col_start
    assert col_start % tn == 0 and m % tn == 0 and n % tm == 0
    first = col_start // tn
    return pl.pallas_call(
        _norm_matmul_kernel,
        grid=(n // tm, m // tn),
        in_specs=[pl.BlockSpec((tm, d), lambda i, j: (i, 0)),
                  pl.BlockSpec((1, d), lambda i, j: (0, 0)),
                  pl.BlockSpec((d, tn), lambda i, j: (0, first + j))],
        out_specs=pl.BlockSpec((tm, tn), lambda i, j: (i, j)),
        out_shape=jax.ShapeDtypeStruct((n, m), out_dtype),
        scratch_shapes=[pltpu.VMEM((tm, d), BF16)],
        compiler_params=_params("parallel", "arbitrary"),
    )(x, g, w)


def _matmul_res_kernel(*refs, n_pairs):
    a_refs, w_refs = refs[:n_pairs], refs[n_pairs:2 * n_pairs]
    x_ref, o_ref = refs[2 * n_pairs], refs[2 * n_pairs + 1]
    acc = x_ref[...]
    for a_ref, w_ref in zip(a_refs, w_refs):
        acc = acc + jnp.dot(a_ref[...], w_ref[...], preferred_element_type=F32)
    o_ref[...] = acc


def _matmul_res(a_list, w, x, tm, tn):
    n, m = x.shape
    n_pairs = len(a_list)
    widths = [a.shape[1] for a in a_list]
    assert all(k == widths[0] for k in widths) and sum(widths) == w.shape[0]
    in_specs = ([pl.BlockSpec((tm, k), lambda i, j: (i, 0)) for k in widths]
                + [pl.BlockSpec((k, tn), functools.partial(lambda i, j, p: (p, j), p=p)) for p, k in enumerate(widths)]
                + [pl.BlockSpec((tm, tn), lambda i, j: (i, j))])
    return pl.pallas_call(
        functools.partial(_matmul_res_kernel, n_pairs=n_pairs),
        grid=(n // tm, m // tn),
        in_specs=in_specs,
        out_specs=pl.BlockSpec((tm, tn), lambda i, j: (i, j)),
        out_shape=jax.ShapeDtypeStruct((n, m), F32),
        compiler_params=_params("parallel", "arbitrary"),
    )(*a_list, *([w] * n_pairs), x)


def _conformer_kernel(a_ref, g_ref, w_ref, b_ref, lng_ref, lnb_ref, o_ref, buf_ref, hs_ref, *, tt, halo):
    n_chunks = buf_ref.shape[0]
    width = w_ref.shape[1]

    @pl.when(pl.program_id(1) == 0)
    def _():
        buf_ref[:, 0:halo, :] = jnp.zeros((n_chunks, halo, LANES), F32)

    glu = a_ref[0] * jax.nn.sigmoid(g_ref[0])
    for c in range(n_chunks):
        buf_ref[c, halo:halo + tt, :] = glu[:, c * LANES:(c + 1) * LANES]

    def chunk_body(c, carry):
        acc = jnp.broadcast_to(b_ref[c], (tt, LANES))
        for k in range(width):
            acc = acc + buf_ref[c, pl.ds(halo - (width - 1) + k, tt), :] * w_ref[c, pl.ds(k, 1), :]
        hs_ref[c] = acc
        buf_ref[c, 0:halo, :] = buf_ref[c, tt:tt + halo, :]
        return carry

    lax.fori_loop(0, n_chunks, chunk_body, 0)

    channels = n_chunks * LANES
    total = hs_ref[0]
    for c in range(1, n_chunks):
        total = total + hs_ref[c]
    mu = jnp.sum(total, axis=-1, keepdims=True) * (1.0 / channels)
    sq = jnp.zeros((tt, LANES), F32)
    for c in range(n_chunks):
        d = hs_ref[c] - mu
        sq = sq + d * d
    var = jnp.sum(sq, axis=-1, keepdims=True) * (1.0 / channels)
    inv = lax.rsqrt(var + LN_EPS)
    for c in range(n_chunks):
        y = (hs_ref[c] - mu) * inv * lng_ref[c] + lnb_ref[c]
        o_ref[0, :, c * LANES:(c + 1) * LANES] = (y * jax.nn.sigmoid(y)).astype(o_ref.dtype)


def _conformer(ag, conv_w, conv_b, ln_g, ln_b, tt):
    bsz, t_len, c2 = ag.shape
    ch = c2 // 2
    n_chunks = ch // LANES
    width = conv_w.shape[0]
    halo = -(-(width - 1) // SUBLANES) * SUBLANES
    by_chunk = lambda p: p.reshape(-1, n_chunks, LANES).transpose(1, 0, 2)
    small = lambda rows: pl.BlockSpec((n_chunks, rows, LANES), lambda b, t: (0, 0, 0))
    return pl.pallas_call(
        functools.partial(_conformer_kernel, tt=tt, halo=halo),
        grid=(bsz, t_len // tt),
        in_specs=[pl.BlockSpec((1, tt, ch), lambda b, t: (b, t, 0)),
                  pl.BlockSpec((1, tt, ch), lambda b, t: (b, t, 1)),
                  small(width), small(1), small(1), small(1)],
        out_specs=pl.BlockSpec((1, tt, ch), lambda b, t: (b, t, 0)),
        out_shape=jax.ShapeDtypeStruct((bsz, t_len, ch), BF16),
        scratch_shapes=[pltpu.VMEM((n_chunks, halo + tt, LANES), F32),
                        pltpu.VMEM((n_chunks, tt, LANES), F32)],
        compiler_params=_params("parallel", "arbitrary"),
    )(ag, ag, by_chunk(conv_w), by_chunk(conv_b), by_chunk(ln_g), by_chunk(ln_b))


def _moba_kernel(*refs, blk, topk, key_tile, n_casts):
    q_ref, k_ref, v_ref = refs[:3]
    o_ref = refs[3 + n_casts]
    kaug_ref, kmean_ref = refs[-2:]
    _run_casts(refs[3:3 + n_casts], refs[4 + n_casts:4 + 2 * n_casts])
    qi = pl.program_id(2)
    t_len = k_ref.shape[1]
    nb = t_len // blk
    heads = [slice(h * HEAD_DIM, (h + 1) * HEAD_DIM) for h in range(q_ref.shape[2] // HEAD_DIM)]
    c2 = HEAD_DIM ** -0.5 * LOG2_E

    @pl.when(qi == 0)
    def _():
        row = lax.broadcasted_iota(jnp.int32, (t_len, LANES), 0)
        col = lax.broadcasted_iota(jnp.int32, (t_len, LANES), 1)
        block_id = jnp.where(row // blk == col, 1.0, 0.0).astype(BF16)
        for h, hd in enumerate(heads):
            k = k_ref[0, :, hd]
            kaug_ref[h, :, 0:HEAD_DIM] = k
            kaug_ref[h, :, HEAD_DIM:HEAD_DIM + LANES] = block_id
            kmean = jnp.mean(k.astype(F32).reshape(nb, blk, HEAD_DIM), axis=1)
            kmean_ref[h] = kmean.astype(BF16)

    own = pl.ds(pl.multiple_of(qi * blk, blk), blk)
    r_i = lax.broadcasted_iota(jnp.int32, (blk, blk), 0)
    c_i = lax.broadcasted_iota(jnp.int32, (blk, blk), 1)
    causal = c_i <= r_i
    blk_id = lax.broadcasted_iota(jnp.int32, (nb, blk), 0)
    blk_id_f = blk_id.astype(F32)
    past = blk_id < qi

    def head_start(h):
        q = q_ref[0, :, heads[h]]
        gate = lax.dot_general(kmean_ref[h], q, _NT, preferred_element_type=F32)
        g = jnp.where(past, gate, -jnp.inf)
        allowed = jnp.zeros(gate.shape, jnp.bool_)
        for _ in range(topk):
            best = jnp.max(g, axis=0, keepdims=True)
            first = jnp.min(jnp.where(g == best, blk_id_f, float(nb)), axis=0, keepdims=True)
            pick = blk_id_f == first
            allowed = jnp.logical_or(allowed, jnp.logical_and(pick, past))
            g = jnp.where(pick, -jnp.inf, g)
        penalty_t = jnp.concatenate([jnp.where(allowed, 0.0, MASKED), jnp.zeros((LANES - nb, blk), F32)], axis=0)
        q_aug = jnp.concatenate([q, penalty_t.T.astype(BF16)], axis=1)
        s = lax.dot_general(q, k_ref[0, own, heads[h]], _NT, preferred_element_type=F32)
        s = jnp.where(causal, s, MASKED)
        m = jnp.max(s, axis=1, keepdims=True)
        p = jnp.exp2((s - m) * c2)
        l = jnp.sum(p, axis=1, keepdims=True)
        acc = jnp.dot(p.astype(BF16), v_ref[0, own, heads[h]], preferred_element_type=F32)
        return q_aug, (m, l, acc)

    started = [head_start(h) for h in range(len(heads))]
    q_augs = [qa for qa, _ in started]

    def body(j, carry):
        ks = pl.ds(pl.multiple_of(j * key_tile, key_tile), key_tile)
        out = []
        for h, (m, l, acc) in enumerate(carry):
            s = lax.dot_general(q_augs[h], kaug_ref[h, ks, :], _NT, preferred_element_type=F32)
            m_new = jnp.maximum(m, jnp.max(s, axis=1, keepdims=True))
            alpha = jnp.exp2((m - m_new) * c2)
            p = jnp.exp2((s - m_new) * c2)
            l = alpha * l + jnp.sum(p, axis=1, keepdims=True)
            acc = alpha * acc + jnp.dot(p.astype(BF16), v_ref[0, ks, heads[h]], preferred_element_type=F32)
            out.append((m_new, l, acc))
        return out

    n_tiles = (qi * blk + key_tile - 1) // key_tile
    final = lax.fori_loop(0, n_tiles, body, [st for _, st in started])
    for h, (m, l, acc) in enumerate(final):
        o_ref[0, :, heads[h]] = (acc / l).astype(o_ref.dtype)


def _attention_grid(qkv, n_heads, tile, heads_per_step):
    bsz, t_len, _ = qkv.shape
    assert n_heads % heads_per_step == 0 and t_len % tile == 0
    groups = n_heads // heads_per_step
    cols = heads_per_step * HEAD_DIM
    q_spec = pl.BlockSpec((1, tile, cols), lambda b, h, i: (b, i, h))
    k_spec = pl.BlockSpec((1, t_len, cols), lambda b, h, i: (b, 0, groups + h))
    v_spec = pl.BlockSpec((1, t_len, cols), lambda b, h, i: (b, 0, 2 * groups + h))
    out_shape = jax.ShapeDtypeStruct((bsz, t_len, n_heads * HEAD_DIM), BF16)
    return (bsz, groups, t_len // tile), [q_spec, k_spec, v_spec], q_spec, out_shape


def _moba(qkv, n_heads, blk, topk, key_tile, heads_per_step, casts):
    t_len = qkv.shape[1]
    assert t_len % key_tile == 0 and key_tile % blk == 0
    assert (t_len // blk) % SUBLANES == 0 and t_len // blk <= LANES
    grid, in_specs, out_spec, out_shape = _attention_grid(qkv, n_heads, blk, heads_per_step)
    c_in, c_out, c_shapes, c_ops = _cast_specs(casts, grid)
    return pl.pallas_call(
        functools.partial(_moba_kernel, blk=blk, topk=topk, key_tile=key_tile, n_casts=len(casts)),
        grid=grid,
        in_specs=in_specs + c_in,
        out_specs=[out_spec, *c_out],
        out_shape=[out_shape, *c_shapes],
        scratch_shapes=[pltpu.VMEM((heads_per_step, t_len, HEAD_DIM + LANES), BF16),
                        pltpu.VMEM((heads_per_step, t_len // blk, HEAD_DIM), BF16)],
        compiler_params=_params("arbitrary", "arbitrary", "arbitrary"),
    )(qkv, qkv, qkv, *c_ops)


def _sb_kernel(*refs, tile, n_casts):
    q_ref, k_ref, v_ref = refs[:3]
    o_ref = refs[3 + n_casts]
    _run_casts(refs[3:3 + n_casts], refs[4 + n_casts:])
    qi = pl.program_id(2)
    scale = HEAD_DIM ** -0.5
    heads = [slice(h * HEAD_DIM, (h + 1) * HEAD_DIM) for h in range(q_ref.shape[2] // HEAD_DIM)]
    neg_q = [-q_ref[0, :, hd] for hd in heads]
    r_i = lax.broadcasted_iota(jnp.int32, (tile, tile), 0)
    c_i = lax.broadcasted_iota(jnp.int32, (tile, tile), 1)
    later = jnp.where(r_i > c_i, 1.0, 0.0).astype(BF16)
    later2 = jnp.concatenate([later, later], axis=0)
    strict = c_i < r_i

    def tile_terms(h, j, diagonal):
        ks = pl.ds(pl.multiple_of(j * tile, tile), tile)
        nz = lax.dot_general(neg_q[h], k_ref[0, ks, heads[h]], _NT, preferred_element_type=F32) * scale
        log_1m = jnp.minimum(nz, 0.0) - jnp.log(1.0 + jnp.exp2(jnp.abs(nz) * -LOG2_E))
        if diagonal:
            log_1m = jnp.where(strict, log_1m, 0.0)
        hi = log_1m.astype(BF16)
        lo = (log_1m - hi.astype(F32)).astype(BF16)
        after = jnp.dot(jnp.concatenate([hi, lo], axis=1), later2, preferred_element_type=F32)
        return ks, log_1m - nz + after, jnp.sum(log_1m, axis=1, keepdims=True)

    def tile_update(h, terms, run, acc, diagonal):
        ks, log_w, total = terms
        a = jnp.exp(log_w + run)
        if diagonal:
            a = jnp.where(strict, a, 0.0)
        acc = acc + jnp.dot(a.astype(BF16), v_ref[0, ks, heads[h]], preferred_element_type=F32)
        return run + total, acc

    n_heads = len(heads)
    diag_terms = [tile_terms(h, qi, True) for h in range(n_heads)]
    prev_terms = [tile_terms(h, jnp.maximum(qi - 1, 0), False) for h in range(n_heads)]
    first_tile = jnp.where(qi > 0, 0.0, MASKED)
    runs, accs = [], []
    for h in range(n_heads):
        run, acc = tile_update(h, diag_terms[h], jnp.zeros((tile, 1), F32), jnp.zeros((tile, HEAD_DIM), F32), True)
        run, acc = tile_update(h, prev_terms[h], run + first_tile, acc, False)
        runs.append(run)
        accs.append(acc)

    def live(runs):
        return jnp.max(functools.reduce(jnp.maximum, runs)) > SB_EXP_IS_ZERO

    def cond(carry):
        j, go, _, _ = carry
        return jnp.logical_and(j >= 0, go)

    def body(carry):
        j, _, runs, accs = carry
        out = [tile_update(h, tile_terms(h, j, False), runs[h], accs[h], False) for h in range(n_heads)]
        runs, accs = [r for r, _ in out], [a for _, a in out]
        return j - 1, live(runs), runs, accs

    _, _, runs, accs = lax.while_loop(cond, body, (qi - 2, live(runs), runs, accs))
    for h in range(n_heads):
        o_ref[0, :, heads[h]] = accs[h].astype(o_ref.dtype)


def _stick_breaking(qkv, n_heads, tile, heads_per_step, casts):
    grid, in_specs, out_spec, out_shape = _attention_grid(qkv, n_heads, tile, heads_per_step)
    c_in, c_out, c_shapes, c_ops = _cast_specs(casts, grid)
    return pl.pallas_call(
        functools.partial(_sb_kernel, tile=tile, n_casts=len(casts)),
        grid=grid,
        in_specs=in_specs + c_in,
        out_specs=[out_spec, *c_out],
        out_shape=[out_shape, *c_shapes],
        compiler_params=_params("arbitrary", "arbitrary", "arbitrary"),
    )(qkv, qkv, qkv, *c_ops)


def _ffn_kernel(*refs, tm, tiles_per_seq, final, n_chains):
    if final:
        x_ref, g_ref, wu_ref, wg_ref, cw_ref, cb_ref, wd_ref, fg_ref, o_ref, h_ref, ubuf_ref, halo_ref = refs
    else:
        x_ref, g_ref, wu_ref, wg_ref, cw_ref, cb_ref, wd_ref, o_ref, h_ref, ubuf_ref, halo_ref = refs
    i = pl.program_id(0)
    f = pl.program_id(1)
    width = cw_ref.shape[0]

    @pl.when(f == 0)
    def _():
        x = x_ref[...]
        h_ref[...] = _rms(x, g_ref[...]).astype(BF16)
        o_ref[...] = x

        @pl.when(i == 0)
        def _():
            halo_ref[...] = jnp.zeros(halo_ref.shape, F32)

    h = h_ref[...]
    seq_start = (i % tiles_per_seq) == 0
    tf = wu_ref.shape[1]
    chunk = tf // n_chains
    acts = []
    for c in range(n_chains):
        cols = slice(c * chunk, (c + 1) * chunk)
        up = jnp.dot(h, wu_ref[:, cols], preferred_element_type=F32)
        gate = jnp.dot(h, wg_ref[:, cols], preferred_element_type=F32)
        ubuf_ref[0:SUBLANES, cols] = jnp.where(seq_start, 0.0, halo_ref[f, :, cols])
        ubuf_ref[SUBLANES:SUBLANES + tm, cols] = up
        halo_ref[f, :, cols] = up[tm - SUBLANES:tm, :]
        conv = up * cw_ref[width - 1:width, cols] + cb_ref[:, cols]
        for k in range(width - 1):
            conv = conv + ubuf_ref[pl.ds(SUBLANES - (width - 1) + k, tm), cols] * cw_ref[k:k + 1, cols]
        acts.append((conv * jax.nn.sigmoid(conv) * gate).astype(BF16))
    o_ref[...] += jnp.dot(jnp.concatenate(acts, axis=1), wd_ref[...], preferred_element_type=F32)

    if final:
        @pl.when(f == pl.num_programs(1) - 1)
        def _():
            o_ref[...] = _rms(o_ref[...], fg_ref[...])


def _ffn(x, g, w_up, w_gate, conv_w, conv_b, w_down, final_g, t_len, tm, tf, n_chains):
    n, d = x.shape
    d_ff = w_up.shape[1]
    width = conv_w.shape[0]
    final = final_g is not None
    row = lambda i, f: (i, 0)
    in_specs = [pl.BlockSpec((tm, d), row),
                pl.BlockSpec((1, d), lambda i, f: (0, 0)),
                pl.BlockSpec((d, tf), lambda i, f: (0, f)),
                pl.BlockSpec((d, tf), lambda i, f: (0, f)),
                pl.BlockSpec((width, tf), lambda i, f: (0, f)),
                pl.BlockSpec((1, tf), lambda i, f: (0, f)),
                pl.BlockSpec((tf, d), lambda i, f: (f, 0))]
    args = [x, g, w_up, w_gate, conv_w, conv_b, w_down]
    if final:
        in_specs.append(pl.BlockSpec((1, d), lambda i, f: (0, 0)))
        args.append(final_g)
    return pl.pallas_call(
        functools.partial(_ffn_kernel, tm=tm, tiles_per_seq=t_len // tm, final=final, n_chains=n_chains),
        grid=(n // tm, d_ff // tf),
        in_specs=in_specs,
        out_specs=pl.BlockSpec((tm, d), row),
        out_shape=jax.ShapeDtypeStruct((n, d), F32),
        scratch_shapes=[pltpu.VMEM((tm, d), BF16),
                        pltpu.VMEM((SUBLANES + tm, tf), F32),
                        pltpu.VMEM((d_ff // tf, SUBLANES, tf), F32)],
        compiler_params=_params("arbitrary", "arbitrary"),
    )(*args)


def kernel(x, mix_norm, ffn_norm, even_w_in, even_conv_w, even_conv_b, even_ln_g, even_ln_b, even_w_out,
           odd_w_qkv, odd_w_o, ffn_w_up, ffn_w_gate, ffn_conv_w, ffn_conv_b, ffn_w_down, final_norm):
    bsz, t_len, d = x.shape
    n = bsz * t_len
    depth = mix_norm.shape[0]
    xs = x.reshape(n, d)

    def later_weights(layer, steps):
        def rows(r):
            return next(rb for rb in range(2 * SUBLANES, r + 1, 2 * SUBLANES) if r % rb == 0 and r // rb <= steps)
        stacks = [(even_w_out if layer % 2 == 0 else odd_w_o, layer // 2),
                  (ffn_w_up, layer), (ffn_w_gate, layer), (ffn_w_down, layer)]
        if layer + 1 < depth:
            stacks.append((odd_w_qkv if layer % 2 == 0 else even_w_in, (layer + 1) // 2))
        return [(w, idx, rows(w.shape[1])) for w, idx in stacks]

    w_first = None
    for layer in range(depth):
        j = layer // 2
        g = mix_norm[layer][None, :]
        if layer % 2 == 0:
            w_in = even_w_in[j].astype(BF16) if w_first is None else w_first
            ag = _norm_matmul(xs, g, w_in, F32, 1024, 2048, 0, 2 * CONV_CH)
            qkv = _norm_matmul(xs, g, w_in, BF16, 1024, 1024, 2 * CONV_CH).reshape(bsz, t_len, -1)
            y_a = _conformer(ag.reshape(bsz, t_len, -1), even_conv_w[j], even_conv_b[j][None, :],
                             even_ln_g[j][None, :], even_ln_b[j][None, :], 256)
            steps = bsz * (MOBA_HEADS // MOBA_HEADS_PER_STEP) * (t_len // MOBA_BLOCK)
            y_b, w_proj, w_up, w_gate, w_down, *w_next = _moba(
                qkv, MOBA_HEADS, MOBA_BLOCK, MOBA_TOPK, MOBA_KEY_TILE, MOBA_HEADS_PER_STEP, later_weights(layer, steps))
            xs = _matmul_res([y_a.reshape(n, -1), y_b.reshape(n, -1)], w_proj, xs, 1024, 1024)
        else:
            w_qkv = odd_w_qkv[j].astype(BF16) if w_first is None else w_first
            qkv = _norm_matmul(xs, g, w_qkv, BF16, 1024, 2048)
            steps = bsz * (SB_HEADS // SB_HEADS_PER_STEP) * (t_len // SB_TILE)
            y, w_proj, w_up, w_gate, w_down, *w_next = _stick_breaking(
                qkv.reshape(bsz, t_len, -1), SB_HEADS, SB_TILE, SB_HEADS_PER_STEP, later_weights(layer, steps))
            xs = _matmul_res([y.reshape(n, -1)], w_proj, xs, 1024, 1024)
        w_first = w_next[0] if w_next else None
        final_g = final_norm[None, :] if layer == depth - 1 else None
        xs = _ffn(xs, ffn_norm[layer][None, :], w_up, w_gate, ffn_conv_w[layer], ffn_conv_b[layer][None, :],
                  w_down, final_g, t_len, 1024, 512, 2)
    return xs.reshape(bsz, t_len, d)
```

```python
import functools

import jax
import jax.numpy as jnp
from jax import lax
from jax.experimental import pallas as pl
from jax.experimental.pallas import tpu as pltpu

F32 = jnp.float32
BF16 = jnp.bfloat16

HEAD_DIM = 128
CONV_CH = 1024
CONV_WIDTH = 31
MOBA_HEADS = 8
MOBA_BLOCK = 256
MOBA_TOPK = 3
MOBA_KEY_TILE = 1024
MOBA_HEADS_PER_STEP = 4
SB_HEADS = 16
SB_TILE = 256
SB_HEADS_PER_STEP = 8
RMS_EPS = 1e-6
LN_EPS = 1e-5
LANES = 128
SUBLANES = 8
MASKED = -1e30
LOG2_E = 1.4426950408889634
SB_EXP_IS_ZERO = -105.0
VMEM_LIMIT = 58 * 1024 * 1024

_NT = (((1,), (1,)), ((), ()))


def _params(*semantics):
    return pltpu.CompilerParams(dimension_semantics=semantics, vmem_limit_bytes=VMEM_LIMIT)


def _rms(x, g):
    ms = jnp.mean(x * x, axis=-1, keepdims=True)
    return x * lax.rsqrt(ms + RMS_EPS) * g


def _cast_specs(casts, grid):
    def step(idx):
        s = idx[0]
        for extent, i in zip(grid[1:], idx[1:]):
            s = s * extent + i
        return s

    n_steps = functools.reduce(lambda a, b: a * b, grid)
    in_specs, out_specs, out_shapes, operands = [], [], [], []
    for stack, layer, rb in casts:
        _, r, c = stack.shape
        assert r % rb == 0 and r // rb <= n_steps
        last = r // rb - 1
        in_specs.append(pl.BlockSpec((None, rb, c), functools.partial(
            lambda *idx, layer, last: (layer, jnp.minimum(step(idx), last), 0), layer=layer, last=last)))
        out_specs.append(pl.BlockSpec((rb, c), functools.partial(
            lambda *idx, last: (jnp.minimum(step(idx), last), 0), last=last)))
        out_shapes.append(jax.ShapeDtypeStruct((r, c), BF16))
        operands.append(stack)
    return in_specs, out_specs, out_shapes, operands


def _run_casts(src_refs, dst_refs):
    for src_ref, dst_ref in zip(src_refs, dst_refs):
        dst_ref[...] = src_ref[...].astype(BF16)


def _norm_matmul_kernel(x_ref, g_ref, w_ref, o_ref, h_ref):
    @pl.when(pl.program_id(1) == 0)
    def _():
        h_ref[...] = _rms(x_ref[...], g_ref[...]).astype(BF16)

    o_ref[...] = jnp.dot(h_ref[...], w_ref[...], preferred_element_type=F32).astype(o_ref.dtype)


def _norm_matmul(x, g, w, out_dtype, tm, tn, col_start=0, col_stop=None):
    n, d = x.shape
    col_stop = w.shape[1] if col_stop is None else col_stop
    m = col_stop - col_start
    assert col_start % tn == 0 and m % tn == 0 and n % tm == 0
    first = col_start // tn
    return pl.pallas_call(
        _norm_matmul_kernel,
        grid=(n // tm, m // tn),
        in_specs=[pl.BlockSpec((tm, d), lambda i, j: (i, 0)),
                  pl.BlockSpec((1, d), lambda i, j: (0, 0)),
                  pl.BlockSpec((d, tn), lambda i, j: (0, first + j))],
        out_specs=pl.BlockSpec((tm, tn), lambda i, j: (i, j)),
        out_shape=jax.ShapeDtypeStruct((n, m), out_dtype),
        scratch_shapes=[pltpu.VMEM((tm, d), BF16)],
        compiler_params=_params("parallel", "arbitrary"),
    )(x, g, w)


def _matmul_res_kernel(*refs, n_pairs):
    a_refs, w_refs = refs[:n_pairs], refs[n_pairs:2 * n_pairs]
    x_ref, o_ref = refs[2 * n_pairs], refs[2 * n_pairs + 1]
    acc = x_ref[...]
    for a_ref, w_ref in zip(a_refs, w_refs):
        acc = acc + jnp.dot(a_ref[...], w_ref[...], preferred_element_type=F32)
    o_ref[...] = acc


def _matmul_res(a_list, w, x, tm, tn):
    n, m = x.shape
    n_pairs = len(a_list)
    widths = [a.shape[1] for a in a_list]
    assert all(k == widths[0] for k in widths) and sum(widths) == w.shape[0]
    in_specs = ([pl.BlockSpec((tm, k), lambda i, j: (i, 0)) for k in widths]
                + [pl.BlockSpec((k, tn), functools.partial(lambda i, j, p: (p, j), p=p)) for p, k in enumerate(widths)]
                + [pl.BlockSpec((tm, tn), lambda i, j: (i, j))])
    return pl.pallas_call(
        functools.partial(_matmul_res_kernel, n_pairs=n_pairs),
        grid=(n // tm, m // tn),
        in_specs=in_specs,
        out_specs=pl.BlockSpec((tm, tn), lambda i, j: (i, j)),
        out_shape=jax.ShapeDtypeStruct((n, m), F32),
        compiler_params=_params("parallel", "arbitrary"),
    )(*a_list, *([w] * n_pairs), x)


def _conformer_kernel(a_ref, g_ref, w_ref, b_ref, lng_ref, lnb_ref, o_ref, buf_ref, hs_ref, *, tt, halo):
    n_chunks = buf_ref.shape[0]
    width = w_ref.shape[1]

    @pl.when(pl.program_id(1) == 0)
    def _():
        buf_ref[:, 0:halo, :] = jnp.zeros((n_chunks, halo, LANES), F32)

    glu = a_ref[0] * jax.nn.sigmoid(g_ref[0])
    for c in range(n_chunks):
        buf_ref[c, halo:halo + tt, :] = glu[:, c * LANES:(c + 1) * LANES]

    def chunk_body(c, carry):
        acc = jnp.broadcast_to(b_ref[c], (tt, LANES))
        for k in range(width):
            acc = acc + buf_ref[c, pl.ds(halo - (width - 1) + k, tt), :] * w_ref[c, pl.ds(k, 1), :]
        hs_ref[c] = acc
        buf_ref[c, 0:halo, :] = buf_ref[c, tt:tt + halo, :]
        return carry

    lax.fori_loop(0, n_chunks, chunk_body, 0)

    channels = n_chunks * LANES
    total = hs_ref[0]
    for c in range(1, n_chunks):
        total = total + hs_ref[c]
    mu = jnp.sum(total, axis=-1, keepdims=True) * (1.0 / channels)
    sq = jnp.zeros((tt, LANES), F32)
    for c in range(n_chunks):
        d = hs_ref[c] - mu
        sq = sq + d * d
    var = jnp.sum(sq, axis=-1, keepdims=True) * (1.0 / channels)
    inv = lax.rsqrt(var + LN_EPS)
    for c in range(n_chunks):
        y = (hs_ref[c] - mu) * inv * lng_ref[c] + lnb_ref[c]
        o_ref[0, :, c * LANES:(c + 1) * LANES] = (y * jax.nn.sigmoid(y)).astype(o_ref.dtype)


def _conformer(ag, conv_w, conv_b, ln_g, ln_b, tt):
    bsz, t_len, c2 = ag.shape
    ch = c2 // 2
    n_chunks = ch // LANES
    width = conv_w.shape[0]
    halo = -(-(width - 1) // SUBLANES) * SUBLANES
    by_chunk = lambda p: p.reshape(-1, n_chunks, LANES).transpose(1, 0, 2)
    small = lambda rows: pl.BlockSpec((n_chunks, rows, LANES), lambda b, t: (0, 0, 0))
    return pl.pallas_call(
        functools.partial(_conformer_kernel, tt=tt, halo=halo),
        grid=(bsz, t_len // tt),
        in_specs=[pl.BlockSpec((1, tt, ch), lambda b, t: (b, t, 0)),
                  pl.BlockSpec((1, tt, ch), lambda b, t: (b, t, 1)),
                  small(width), small(1), small(1), small(1)],
        out_specs=pl.BlockSpec((1, tt, ch), lambda b, t: (b, t, 0)),
        out_shape=jax.ShapeDtypeStruct((bsz, t_len, ch), BF16),
        scratch_shapes=[pltpu.VMEM((n_chunks, halo + tt, LANES), F32),
                        pltpu.VMEM((n_chunks, tt, LANES), F32)],
        compiler_params=_params("parallel", "arbitrary"),
    )(ag, ag, by_chunk(conv_w), by_chunk(conv_b), by_chunk(ln_g), by_chunk(ln_b))


def _moba_kernel(*refs, blk, topk, key_tile, n_casts):
    q_ref, k_ref, v_ref = refs[:3]
    o_ref = refs[3 + n_casts]
    kaug_ref, kmean_ref = refs[-2:]
    _run_casts(refs[3:3 + n_casts], refs[4 + n_casts:4 + 2 * n_casts])
    qi = pl.program_id(2)
    t_len = k_ref.shape[1]
    nb = t_len // blk
    heads = [slice(h * HEAD_DIM, (h + 1) * HEAD_DIM) for h in range(q_ref.shape[2] // HEAD_DIM)]
    c2 = HEAD_DIM ** -0.5 * LOG2_E

    @pl.when(qi == 0)
    def _():
        row = lax.broadcasted_iota(jnp.int32, (t_len, LANES), 0)
        col = lax.broadcasted_iota(jnp.int32, (t_len, LANES), 1)
        block_id = jnp.where(row // blk == col, 1.0, 0.0).astype(BF16)
        for h, hd in enumerate(heads):
            k = k_ref[0, :, hd]
            kaug_ref[h, :, 0:HEAD_DIM] = k
            kaug_ref[h, :, HEAD_DIM:HEAD_DIM + LANES] = block_id
            kmean = jnp.mean(k.astype(F32).reshape(nb, blk, HEAD_DIM), axis=1)
            kmean_ref[h] = kmean.astype(BF16)

    own = pl.ds(pl.multiple_of(qi * blk, blk), blk)
    r_i = lax.broadcasted_iota(jnp.int32, (blk, blk), 0)
    c_i = lax.broadcasted_iota(jnp.int32, (blk, blk), 1)
    causal = c_i <= r_i
    blk_id = lax.broadcasted_iota(jnp.int32, (nb, blk), 0)
    blk_id_f = blk_id.astype(F32)
    past = blk_id < qi

    def head_start(h):
        q = q_ref[0, :, heads[h]]
        gate = lax.dot_general(kmean_ref[h], q, _NT, preferred_element_type=F32)
        g = jnp.where(past, gate, -jnp.inf)
        allowed = jnp.zeros(gate.shape, jnp.bool_)
        for _ in range(topk):
            best = jnp.max(g, axis=0, keepdims=True)
            first = jnp.min(jnp.where(g == best, blk_id_f, float(nb)), axis=0, keepdims=True)
            pick = blk_id_f == first
            allowed = jnp.logical_or(allowed, jnp.logical_and(pick, past))
            g = jnp.where(pick, -jnp.inf, g)
        penalty_t = jnp.concatenate([jnp.where(allowed, 0.0, MASKED), jnp.zeros((LANES - nb, blk), F32)], axis=0)
        q_aug = jnp.concatenate([q, penalty_t.T.astype(BF16)], axis=1)
        s = lax.dot_general(q, k_ref[0, own, heads[h]], _NT, preferred_element_type=F32)
        s = jnp.where(causal, s, MASKED)
        m = jnp.max(s, axis=1, keepdims=True)
        p = jnp.exp2((s - m) * c2)
        l = jnp.sum(p, axis=1, keepdims=True)
        acc = jnp.dot(p.astype(BF16), v_ref[0, own, heads[h]], preferred_element_type=F32)
        return q_aug, (m, l, acc)

    started = [head_start(h) for h in range(len(heads))]
    q_augs = [qa for qa, _ in started]

    def body(j, carry):
        ks = pl.ds(pl.multiple_of(j * key_tile, key_tile), key_tile)
        out = []
        for h, (m, l, acc) in enumerate(carry):
            s = lax.dot_general(q_augs[h], kaug_ref[h, ks, :], _NT, preferred_element_type=F32)
            m_new = jnp.maximum(m, jnp.max(s, axis=1, keepdims=True))
            alpha = jnp.exp2((m - m_new) * c2)
            p = jnp.exp2((s - m_new) * c2)
            l = alpha * l + jnp.sum(p, axis=1, keepdims=True)
            acc = alpha * acc + jnp.dot(p.astype(BF16), v_ref[0, ks, heads[h]], preferred_element_type=F32)
            out.append((m_new, l, acc))
        return out

    n_tiles = (qi * blk + key_tile - 1) // key_tile
    final = lax.fori_loop(0, n_tiles, body, [st for _, st in started])
    for h, (m, l, acc) in enumerate(final):
        o_ref[0, :, heads[h]] = (acc / l).astype(o_ref.dtype)


def _attention_grid(qkv, n_heads, tile, heads_per_step):
    bsz, t_len, _ = qkv.shape
    assert n_heads % heads_per_step == 0 and t_len % tile == 0
    groups = n_heads // heads_per_step
    cols = heads_per_step * HEAD_DIM
    q_spec = pl.BlockSpec((1, tile, cols), lambda b, h, i: (b, i, h))
    k_spec = pl.BlockSpec((1, t_len, cols), lambda b, h, i: (b, 0, groups + h))
    v_spec = pl.BlockSpec((1, t_len, cols), lambda b, h, i: (b, 0, 2 * groups + h))
    out_shape = jax.ShapeDtypeStruct((bsz, t_len, n_heads * HEAD_DIM), BF16)
    return (bsz, groups, t_len // tile), [q_spec, k_spec, v_spec], q_spec, out_shape


def _moba(qkv, n_heads, blk, topk, key_tile, heads_per_step, casts):
    t_len = qkv.shape[1]
    assert t_len % key_tile == 0 and key_tile % blk == 0
    assert (t_len // blk) % SUBLANES == 0 and t_len // blk <= LANES
    grid, in_specs, out_spec, out_shape = _attention_grid(qkv, n_heads, blk, heads_per_step)
    c_in, c_out, c_shapes, c_ops = _cast_specs(casts, grid)
    return pl.pallas_call(
        functools.partial(_moba_kernel, blk=blk, topk=topk, key_tile=key_tile, n_casts=len(casts)),
        grid=grid,
        in_specs=in_specs + c_in,
        out_specs=[out_spec, *c_out],
        out_shape=[out_shape, *c_shapes],
        scratch_shapes=[pltpu.VMEM((heads_per_step, t_len, HEAD_DIM + LANES), BF16),
                        pltpu.VMEM((heads_per_step, t_len // blk, HEAD_DIM), BF16)],
        compiler_params=_params("arbitrary", "arbitrary", "arbitrary"),
    )(qkv, qkv, qkv, *c_ops)


def _sb_kernel(*refs, tile, n_casts):
    q_ref, k_ref, v_ref = refs[:3]
    o_ref = refs[3 + n_casts]
    _run_casts(refs[3:3 + n_casts], refs[4 + n_casts:])
    qi = pl.program_id(2)
    scale = HEAD_DIM ** -0.5
    heads = [slice(h * HEAD_DIM, (h + 1) * HEAD_DIM) for h in range(q_ref.shape[2] // HEAD_DIM)]
    neg_q = [-q_ref[0, :, hd] for hd in heads]
    r_i = lax.broadcasted_iota(jnp.int32, (tile, tile), 0)
    c_i = lax.broadcasted_iota(jnp.int32, (tile, tile), 1)
    later = jnp.where(r_i > c_i, 1.0, 0.0).astype(BF16)
    later2 = jnp.concatenate([later, later], axis=0)
    strict = c_i < r_i

    def tile_terms(h, j, diagonal):
        ks = pl.ds(pl.multiple_of(j * tile, tile), tile)
        nz = lax.dot_general(neg_q[h], k_ref[0, ks, heads[h]], _NT, preferred_element_type=F32) * scale
        log_1m = jnp.minimum(nz, 0.0) - jnp.log(1.0 + jnp.exp2(jnp.abs(nz) * -LOG2_E))
        if diagonal:
            log_1m = jnp.where(strict, log_1m, 0.0)
        hi = log_1m.astype(BF16)
        lo = (log_1m - hi.astype(F32)).astype(BF16)
        after = jnp.dot(jnp.concatenate([hi, lo], axis=1), later2, preferred_element_type=F32)
        return ks, log_1m - nz + after, jnp.sum(log_1m, axis=1, keepdims=True)

    def tile_update(h, terms, run, acc, diagonal):
        ks, log_w, total = terms
        a = jnp.exp(log_w + run)
        if diagonal:
            a = jnp.where(strict, a, 0.0)
        acc = acc + jnp.dot(a.astype(BF16), v_ref[0, ks, heads[h]], preferred_element_type=F32)
        return run + total, acc

    n_heads = len(heads)
    diag_terms = [tile_terms(h, qi, True) for h in range(n_heads)]
    prev_terms = [tile_terms(h, jnp.maximum(qi - 1, 0), False) for h in range(n_heads)]
    first_tile = jnp.where(qi > 0, 0.0, MASKED)
    runs, accs = [], []
    for h in range(n_heads):
        run, acc = tile_update(h, diag_terms[h], jnp.zeros((tile, 1), F32), jnp.zeros((tile, HEAD_DIM), F32), True)
        run, acc = tile_update(h, prev_terms[h], run + first_tile, acc, False)
        runs.append(run)
        accs.append(acc)

    def live(runs):
        return jnp.max(functools.reduce(jnp.maximum, runs)) > SB_EXP_IS_ZERO

    def cond(carry):
        j, go, _, _ = carry
        return jnp.logical_and(j >= 0, go)

    def body(carry):
        j, _, runs, accs = carry
        out = [tile_update(h, tile_terms(h, j, False), runs[h], accs[h], False) for h in range(n_heads)]
        runs, accs = [r for r, _ in out], [a for _, a in out]
        return j - 1, live(runs), runs, accs

    _, _, runs, accs = lax.while_loop(cond, body, (qi - 2, live(runs), runs, accs))
    for h in range(n_heads):
        o_ref[0, :, heads[h]] = accs[h].astype(o_ref.dtype)


def _stick_breaking(qkv, n_heads, tile, heads_per_step, casts):
    grid, in_specs, out_spec, out_shape = _attention_grid(qkv, n_heads, tile, heads_per_step)
    c_in, c_out, c_shapes, c_ops = _cast_specs(casts, grid)
    return pl.pallas_call(
        functools.partial(_sb_kernel, tile=tile, n_casts=len(casts)),
        grid=grid,
        in_specs=in_specs + c_in,
        out_specs=[out_spec, *c_out],
        out_shape=[out_shape, *c_shapes],
        compiler_params=_params("arbitrary", "arbitrary", "arbitrary"),
    )(qkv, qkv, qkv, *c_ops)


def _ffn_kernel(*refs, tm, tiles_per_seq, final, n_chains):
    if final:
        x_ref, g_ref, wu_ref, wg_ref, cw_ref, cb_ref, wd_ref, fg_ref, o_ref, h_ref, ubuf_ref, halo_ref = refs
    else:
        x_ref, g_ref, wu_ref, wg_ref, cw_ref, cb_ref, wd_ref, o_ref, h_ref, ubuf_ref, halo_ref = refs
    i = pl.program_id(0)
    f = pl.program_id(1)
    width = cw_ref.shape[0]

    @pl.when(f == 0)
    def _():
        x = x_ref[...]
        h_ref[...] = _rms(x, g_ref[...]).astype(BF16)
        o_ref[...] = x

        @pl.when(i == 0)
        def _():
            halo_ref[...] = jnp.zeros(halo_ref.shape, F32)

    h = h_ref[...]
    seq_start = (i % tiles_per_seq) == 0
    tf = wu_ref.shape[1]
    chunk = tf // n_chains
    acts = []
    for c in range(n_chains):
        cols = slice(c * chunk, (c + 1) * chunk)
        up = jnp.dot(h, wu_ref[:, cols], preferred_element_type=F32)
        gate = jnp.dot(h, wg_ref[:, cols], preferred_element_type=F32)
        ubuf_ref[0:SUBLANES, cols] = jnp.where(seq_start, 0.0, halo_ref[f, :, cols])
        ubuf_ref[SUBLANES:SUBLANES + tm, cols] = up
        halo_ref[f, :, cols] = up[tm - SUBLANES:tm, :]
        conv = up * cw_ref[width - 1:width, cols] + cb_ref[:, cols]
        for k in range(width - 1):
            conv = conv + ubuf_ref[pl.ds(SUBLANES - (width - 1) + k, tm), cols] * cw_ref[k:k + 1, cols]
        acts.append((conv * jax.nn.sigmoid(conv) * gate).astype(BF16))
    o_ref[...] += jnp.dot(jnp.concatenate(acts, axis=1), wd_ref[...], preferred_element_type=F32)

    if final:
        @pl.when(f == pl.num_programs(1) - 1)
        def _():
            o_ref[...] = _rms(o_ref[...], fg_ref[...])


def _ffn(x, g, w_up, w_gate, conv_w, conv_b, w_down, final_g, t_len, tm, tf, n_chains):
    n, d = x.shape
    d_ff = w_up.shape[1]
    width = conv_w.shape[0]
    final = final_g is not None
    row = lambda i, f: (i, 0)
    in_specs = [pl.BlockSpec((tm, d), row),
                pl.BlockSpec((1, d), lambda i, f: (0, 0)),
                pl.BlockSpec((d, tf), lambda i, f: (0, f)),
                pl.BlockSpec((d, tf), lambda i, f: (0, f)),
                pl.BlockSpec((width, tf), lambda i, f: (0, f)),
                pl.BlockSpec((1, tf), lambda i, f: (0, f)),
                pl.BlockSpec((tf, d), lambda i, f: (f, 0))]
    args = [x, g, w_up, w_gate, conv_w, conv_b, w_down]
    if final:
        in_specs.append(pl.BlockSpec((1, d), lambda i, f: (0, 0)))
        args.append(final_g)
    return pl.pallas_call(
        functools.partial(_ffn_kernel, tm=tm, tiles_per_seq=t_len // tm, final=final, n_chains=n_chains),
        grid=(n // tm, d_ff // tf),
        in_specs=in_specs,
        out_specs=pl.BlockSpec((tm, d), row),
        out_shape=jax.ShapeDtypeStruct((n, d), F32),
        scratch_shapes=[pltpu.VMEM((tm, d), BF16),
                        pltpu.VMEM((SUBLANES + tm, tf), F32),
                        pltpu.VMEM((d_ff // tf, SUBLANES, tf), F32)],
        compiler_params=_params("arbitrary", "arbitrary"),
    )(*args)


def kernel(x, mix_norm, ffn_norm, even_w_in, even_conv_w, even_conv_b, even_ln_g, even_ln_b, even_w_out,
           odd_w_qkv, odd_w_o, ffn_w_up, ffn_w_gate, ffn_conv_w, ffn_conv_b, ffn_w_down, final_norm):
    bsz, t_len, d = x.shape
    n = bsz * t_len
    depth = mix_norm.shape[0]
    xs = x.reshape(n, d)

    def later_weights(layer, steps):
        def rows(r):
            return next(rb for rb in range(2 * SUBLANES, r + 1, 2 * SUBLANES) if r % rb == 0 and r // rb <= steps)
        stacks = [(even_w_out if layer % 2 == 0 else odd_w_o, layer // 2),
                  (ffn_w_up, layer), (ffn_w_gate, layer), (ffn_w_down, layer)]
        if layer + 1 < depth:
            stacks.append((odd_w_qkv if layer % 2 == 0 else even_w_in, (layer + 1) // 2))
        return [(w, idx, rows(w.shape[1])) for w, idx in stacks]

    w_first = None
    for layer in range(depth):
        j = layer // 2
        g = mix_norm[layer][None, :]
        if layer % 2 == 0:
            w_in = even_w_in[j].astype(BF16) if w_first is None else w_first
            ag = _norm_matmul(xs, g, w_in, F32, 1024, 2048, 0, 2 * CONV_CH)
            qkv = _norm_matmul(xs, g, w_in, BF16, 1024, 1024, 2 * CONV_CH).reshape(bsz, t_len, -1)
            y_a = _conformer(ag.reshape(bsz, t_len, -1), even_conv_w[j], even_conv_b[j][None, :],
                             even_ln_g[j][None, :], even_ln_b[j][None, :], 256)
            steps = bsz * (MOBA_HEADS // MOBA_HEADS_PER_STEP) * (t_len // MOBA_BLOCK)
            y_b, w_proj, w_up, w_gate, w_down, *w_next = _moba(
                qkv, MOBA_HEADS, MOBA_BLOCK, MOBA_TOPK, MOBA_KEY_TILE, MOBA_HEADS_PER_STEP, later_weights(layer, steps))
            xs = _matmul_res([y_a.reshape(n, -1), y_b.reshape(n, -1)], w_proj, xs, 1024, 1024)
        else:
            w_qkv = odd_w_qkv[j].astype(BF16) if w_first is None else w_first
            qkv = _norm_matmul(xs, g, w_qkv, BF16, 1024, 2048)
            steps = bsz * (SB_HEADS // SB_HEADS_PER_STEP) * (t_len // SB_TILE)
            y, w_proj, w_up, w_gate, w_down, *w_next = _stick_breaking(
                qkv.reshape(bsz, t_len, -1), SB_HEADS, SB_TILE, SB_HEADS_PER_STEP, later_weights(layer, steps))
            xs = _matmul_res([y.reshape(n, -1)], w_proj, xs, 1024, 1024)
        w_first = w_next[0] if w_next else None
        final_g = final_norm[None, :] if layer == depth - 1 else None
        xs = _ffn(xs, ffn_norm[layer][None, :], w_up, w_gate, ffn_conv_w[layer], ffn_conv_b[layer][None, :],
                  w_down, final_g, t_len, 1024, 512, 2)
    return xs.reshape(bsz, t_len, d)
```

```python
import functools

import jax
import jax.numpy as jnp
from jax import lax
from jax.experimental import pallas as pl
from jax.experimental.pallas import tpu as pltpu

F32 = jnp.float32
BF16 = jnp.bfloat16

HEAD_DIM = 128
CONV_CH = 1024
CONV_WIDTH = 31
MOBA_HEADS = 8
MOBA_BLOCK = 256
MOBA_TOPK = 3
MOBA_KEY_TILE = 1024
MOBA_HEADS_PER_STEP = 4
SB_HEADS = 16
SB_TILE = 256
SB_HEADS_PER_STEP = 8
RMS_EPS = 1e-6
LN_EPS = 1e-5
LANES = 128
SUBLANES = 8
MASKED = -1e30
LOG2_E = 1.4426950408889634
SB_EXP_IS_ZERO = -105.0
VMEM_LIMIT = 58 * 1024 * 1024

_NT = (((1,), (1,)), ((), ()))


def _params(*semantics):
    return pltpu.CompilerParams(dimension_semantics=semantics, vmem_limit_bytes=VMEM_LIMIT)


def _rms(x, g):
    ms = jnp.mean(x * x, axis=-1, keepdims=True)
    return x * lax.rsqrt(ms + RMS_EPS) * g


def _cast_specs(casts, grid):
    def step(idx):
        s = idx[0]
        for extent, i in zip(grid[1:], idx[1:]):
            s = s * extent + i
        return s

    n_steps = functools.reduce(lambda a, b: a * b, grid)
    in_specs, out_specs, out_shapes, operands = [], [], [], []
    for stack, layer, rb in casts:
        _, r, c = stack.shape
        assert r % rb == 0 and r // rb <= n_steps
        last = r // rb - 1
        in_specs.append(pl.BlockSpec((None, rb, c), functools.partial(
            lambda *idx, layer, last: (layer, jnp.minimum(step(idx), last), 0), layer=layer, last=last)))
        out_specs.append(pl.BlockSpec((rb, c), functools.partial(
            lambda *idx, last: (jnp.minimum(step(idx), last), 0), last=last)))
        out_shapes.append(jax.ShapeDtypeStruct((r, c), BF16))
        operands.append(stack)
    return in_specs, out_specs, out_shapes, operands


def _run_casts(src_refs, dst_refs):
    for src_ref, dst_ref in zip(src_refs, dst_refs):
        dst_ref[...] = src_ref[...].astype(BF16)


def _norm_matmul_kernel(x_ref, g_ref, w_ref, o_ref, h_ref):
    @pl.when(pl.program_id(1) == 0)
    def _():
        h_ref[...] = _rms(x_ref[...], g_ref[...]).astype(BF16)

    o_ref[...] = jnp.dot(h_ref[...], w_ref[...], preferred_element_type=F32).astype(o_ref.dtype)


def _norm_matmul(x, g, w, out_dtype, tm, tn, col_start=0, col_stop=None):
    n, d = x.shape
    col_stop = w.shape[1] if col_stop is None else col_stop
    m = col_stop - col_start
    assert col_start % tn == 0 and m % tn == 0 and n % tm == 0
    first = col_start // tn
    return pl.pallas_call(
        _norm_matmul_kernel,
        grid=(n // tm, m // tn),
        in_specs=[pl.BlockSpec((tm, d), lambda i, j: (i, 0)),
                  pl.BlockSpec((1, d), lambda i, j: (0, 0)),
                  pl.BlockSpec((d, tn), lambda i, j: (0, first + j))],
        out_specs=pl.BlockSpec((tm, tn), lambda i, j: (i, j)),
        out_shape=jax.ShapeDtypeStruct((n, m), out_dtype),
        scratch_shapes=[pltpu.VMEM((tm, d), BF16)],
        compiler_params=_params("parallel", "arbitrary"),
    )(x, g, w)


def _matmul_res_kernel(*refs, n_pairs):
    a_refs, w_refs = refs[:n_pairs], refs[n_pairs:2 * n_pairs]
    x_ref, o_ref = refs[2 * n_pairs], refs[2 * n_pairs + 1]
    acc = x_ref[...]
    for a_ref, w_ref in zip(a_refs, w_refs):
        acc = acc + jnp.dot(a_ref[...], w_ref[...], preferred_element_type=F32)
    o_ref[...] = acc


def _matmul_res(a_list, w, x, tm, tn):
    n, m = x.shape
    n_pairs = len(a_list)
    widths = [a.shape[1] for a in a_list]
    assert all(k == widths[0] for k in widths) and sum(widths) == w.shape[0]
    in_specs = ([pl.BlockSpec((tm, k), lambda i, j: (i, 0)) for k in widths]
                + [pl.BlockSpec((k, tn), functools.partial(lambda i, j, p: (p, j), p=p)) for p, k in enumerate(widths)]
                + [pl.BlockSpec((tm, tn), lambda i, j: (i, j))])
    return pl.pallas_call(
        functools.partial(_matmul_res_kernel, n_pairs=n_pairs),
        grid=(n // tm, m // tn),
        in_specs=in_specs,
        out_specs=pl.BlockSpec((tm, tn), lambda i, j: (i, j)),
        out_shape=jax.ShapeDtypeStruct((n, m), F32),
        compiler_params=_params("parallel", "arbitrary"),
    )(*a_list, *([w] * n_pairs), x)


def _conformer_kernel(a_ref, g_ref, w_ref, b_ref, lng_ref, lnb_ref, o_ref, buf_ref, hs_ref, *, tt, halo):
    n_chunks = buf_ref.shape[0]
    width = w_ref.shape[1]

    @pl.when(pl.program_id(1) == 0)
    def _():
        buf_ref[:, 0:halo, :] = jnp.zeros((n_chunks, halo, LANES), F32)

    glu = a_ref[0] * jax.nn.sigmoid(g_ref[0])
    for c in range(n_chunks):
        buf_ref[c, halo:halo + tt, :] = glu[:, c * LANES:(c + 1) * LANES]

    def chunk_body(c, carry):
        acc = jnp.broadcast_to(b_ref[c], (tt, LANES))
        for k in range(width):
            acc = acc + buf_ref[c, pl.ds(halo - (width - 1) + k, tt), :] * w_ref[c, pl.ds(k, 1), :]
        hs_ref[c] = acc
        buf_ref[c, 0:halo, :] = buf_ref[c, tt:tt + halo, :]
        return carry

    lax.fori_loop(0, n_chunks, chunk_body, 0)

    channels = n_chunks * LANES
    total = hs_ref[0]
    for c in range(1, n_chunks):
        total = total + hs_ref[c]
    mu = jnp.sum(total, axis=-1, keepdims=True) * (1.0 / channels)
    sq = jnp.zeros((tt, LANES), F32)
    for c in range(n_chunks):
        d = hs_ref[c] - mu
        sq = sq + d * d
    var = jnp.sum(sq, axis=-1, keepdims=True) * (1.0 / channels)
    inv = lax.rsqrt(var + LN_EPS)
    for c in range(n_chunks):
        y = (hs_ref[c] - mu) * inv * lng_ref[c] + lnb_ref[c]
        o_ref[0, :, c * LANES:(c + 1) * LANES] = (y * jax.nn.sigmoid(y)).astype(o_ref.dtype)


def _conformer(ag, conv_w, conv_b, ln_g, ln_b, tt):
    bsz, t_len, c2 = ag.shape
    ch = c2 // 2
    n_chunks = ch // LANES
    width = conv_w.shape[0]
    halo = -(-(width - 1) // SUBLANES) * SUBLANES
    by_chunk = lambda p: p.reshape(-1, n_chunks, LANES).transpose(1, 0, 2)
    small = lambda rows: pl.BlockSpec((n_chunks, rows, LANES), lambda b, t: (0, 0, 0))
    return pl.pallas_call(
        functools.partial(_conformer_kernel, tt=tt, halo=halo),
        grid=(bsz, t_len // tt),
        in_specs=[pl.BlockSpec((1, tt, ch), lambda b, t: (b, t, 0)),
                  pl.BlockSpec((1, tt, ch), lambda b, t: (b, t, 1)),
                  small(width), small(1), small(1), small(1)],
        out_specs=pl.BlockSpec((1, tt, ch), lambda b, t: (b, t, 0)),
        out_shape=jax.ShapeDtypeStruct((bsz, t_len, ch), BF16),
        scratch_shapes=[pltpu.VMEM((n_chunks, halo + tt, LANES), F32),
                        pltpu.VMEM((n_chunks, tt, LANES), F32)],
        compiler_params=_params("parallel", "arbitrary"),
    )(ag, ag, by_chunk(conv_w), by_chunk(conv_b), by_chunk(ln_g), by_chunk(ln_b))


def _moba_kernel(*refs, blk, topk, key_tile, n_casts):
    q_ref, k_ref, v_ref = refs[:3]
    o_ref = refs[3 + n_casts]
    kaug_ref, kmean_ref = refs[-2:]
    _run_casts(refs[3:3 + n_casts], refs[4 + n_casts:4 + 2 * n_casts])
    qi = pl.program_id(2)
    t_len = k_ref.shape[1]
    nb = t_len // blk
    heads = [slice(h * HEAD_DIM, (h + 1) * HEAD_DIM) for h in range(q_ref.shape[2] // HEAD_DIM)]
    c2 = HEAD_DIM ** -0.5 * LOG2_E

    @pl.when(qi == 0)
    def _():
        row = lax.broadcasted_iota(jnp.int32, (t_len, LANES), 0)
        col = lax.broadcasted_iota(jnp.int32, (t_len, LANES), 1)
        block_id = jnp.where(row // blk == col, 1.0, 0.0).astype(BF16)
        for h, hd in enumerate(heads):
            k = k_ref[0, :, hd]
            kaug_ref[h, :, 0:HEAD_DIM] = k
            kaug_ref[h, :, HEAD_DIM:HEAD_DIM + LANES] = block_id
            kmean = jnp.mean(k.astype(F32).reshape(nb, blk, HEAD_DIM), axis=1)
            kmean_ref[h] = kmean.astype(BF16)

    own = pl.ds(pl.multiple_of(qi * blk, blk), blk)
    r_i = lax.broadcasted_iota(jnp.int32, (blk, blk), 0)
    c_i = lax.broadcasted_iota(jnp.int32, (blk, blk), 1)
    causal = c_i <= r_i
    blk_id = lax.broadcasted_iota(jnp.int32, (nb, blk), 0)
    blk_id_f = blk_id.astype(F32)
    past = blk_id < qi

    def head_start(h):
        q = q_ref[0, :, heads[h]]
        gate = lax.dot_general(kmean_ref[h], q, _NT, preferred_element_type=F32)
        g = jnp.where(past, gate, -jnp.inf)
        allowed = jnp.zeros(gate.shape, jnp.bool_)
        for _ in range(topk):
            best = jnp.max(g, axis=0, keepdims=True)
            first = jnp.min(jnp.where(g == best, blk_id_f, float(nb)), axis=0, keepdims=True)
            pick = blk_id_f == first
            allowed = jnp.logical_or(allowed, jnp.logical_and(pick, past))
            g = jnp.where(pick, -jnp.inf, g)
        penalty_t = jnp.concatenate([jnp.where(allowed, 0.0, MASKED), jnp.zeros((LANES - nb, blk), F32)], axis=0)
        q_aug = jnp.concatenate([q, penalty_t.T.astype(BF16)], axis=1)
        s = lax.dot_general(q, k_ref[0, own, heads[h]], _NT, preferred_element_type=F32)
        s = jnp.where(causal, s, MASKED)
        m = jnp.max(s, axis=1, keepdims=True)
        p = jnp.exp2((s - m) * c2)
        l = jnp.sum(p, axis=1, keepdims=True)
        acc = jnp.dot(p.astype(BF16), v_ref[0, own, heads[h]], preferred_element_type=F32)
        return q_aug, (m, l, acc)

    started = [head_start(h) for h in range(len(heads))]
    q_augs = [qa for qa, _ in started]

    def body(j, carry):
        ks = pl.ds(pl.multiple_of(j * key_tile, key_tile), key_tile)
        out = []
        for h, (m, l, acc) in enumerate(carry):
            s = lax.dot_general(q_augs[h], kaug_ref[h, ks, :], _NT, preferred_element_type=F32)
            m_new = jnp.maximum(m, jnp.max(s, axis=1, keepdims=True))
            alpha = jnp.exp2((m - m_new) * c2)
            p = jnp.exp2((s - m_new) * c2)
            l = alpha * l + jnp.sum(p, axis=1, keepdims=True)
            acc = alpha * acc + jnp.dot(p.astype(BF16), v_ref[0, ks, heads[h]], preferred_element_type=F32)
            out.append((m_new, l, acc))
        return out

    n_tiles = (qi * blk + key_tile - 1) // key_tile
    final = lax.fori_loop(0, n_tiles, body, [st for _, st in started])
    for h, (m, l, acc) in enumerate(final):
        o_ref[0, :, heads[h]] = (acc / l).astype(o_ref.dtype)


def _attention_grid(qkv, n_heads, tile, heads_per_step):
    bsz, t_len, _ = qkv.shape
    assert n_heads % heads_per_step == 0 and t_len % tile == 0
    groups = n_heads // heads_per_step
    cols = heads_per_step * HEAD_DIM
    q_spec = pl.BlockSpec((1, tile, cols), lambda b, h, i: (b, i, h))
    k_spec = pl.BlockSpec((1, t_len, cols), lambda b, h, i: (b, 0, groups + h))
    v_spec = pl.BlockSpec((1, t_len, cols), lambda b, h, i: (b, 0, 2 * groups + h))
    out_shape = jax.ShapeDtypeStruct((bsz, t_len, n_heads * HEAD_DIM), BF16)
    return (bsz, groups, t_len // tile), [q_spec, k_spec, v_spec], q_spec, out_shape


def _moba(qkv, n_heads, blk, topk, key_tile, heads_per_step, casts):
    t_len = qkv.shape[1]
    assert t_len % key_tile == 0 and key_tile % blk == 0
    assert (t_len // blk) % SUBLANES == 0 and t_len // blk <= LANES
    grid, in_specs, out_spec, out_shape = _attention_grid(qkv, n_heads, blk, heads_per_step)
    c_in, c_out, c_shapes, c_ops = _cast_specs(casts, grid)
    return pl.pallas_call(
        functools.partial(_moba_kernel, blk=blk, topk=topk, key_tile=key_tile, n_casts=len(casts)),
        grid=grid,
        in_specs=in_specs + c_in,
        out_specs=[out_spec, *c_out],
        out_shape=[out_shape, *c_shapes],
        scratch_shapes=[pltpu.VMEM((heads_per_step, t_len, HEAD_DIM + LANES), BF16),
                        pltpu.VMEM((heads_per_step, t_len // blk, HEAD_DIM), BF16)],
        compiler_params=_params("arbitrary", "arbitrary", "arbitrary"),
    )(qkv, qkv, qkv, *c_ops)


def _sb_kernel(*refs, tile, n_casts):
    q_ref, k_ref, v_ref = refs[:3]
    o_ref = refs[3 + n_casts]
    _run_casts(refs[3:3 + n_casts], refs[4 + n_casts:])
    qi = pl.program_id(2)
    scale = HEAD_DIM ** -0.5
    heads = [slice(h * HEAD_DIM, (h + 1) * HEAD_DIM) for h in range(q_ref.shape[2] // HEAD_DIM)]
    neg_q = [-q_ref[0, :, hd] for hd in heads]
    r_i = lax.broadcasted_iota(jnp.int32, (tile, tile), 0)
    c_i = lax.broadcasted_iota(jnp.int32, (tile, tile), 1)
    later = jnp.where(r_i > c_i, 1.0, 0.0).astype(BF16)
    later2 = jnp.concatenate([later, later], axis=0)
    strict = c_i < r_i

    def tile_terms(h, j, diagonal):
        ks = pl.ds(pl.multiple_of(j * tile, tile), tile)
        nz = lax.dot_general(neg_q[h], k_ref[0, ks, heads[h]], _NT, preferred_element_type=F32) * scale
        log_1m = jnp.minimum(nz, 0.0) - jnp.log(1.0 + jnp.exp2(jnp.abs(nz) * -LOG2_E))
        if diagonal:
            log_1m = jnp.where(strict, log_1m, 0.0)
        hi = log_1m.astype(BF16)
        lo = (log_1m - hi.astype(F32)).astype(BF16)
        after = jnp.dot(jnp.concatenate([hi, lo], axis=1), later2, preferred_element_type=F32)
        return ks, log_1m - nz + after, jnp.sum(log_1m, axis=1, keepdims=True)

    def tile_update(h, terms, run, acc, diagonal):
        ks, log_w, total = terms
        a = jnp.exp(log_w + run)
        if diagonal:
            a = jnp.where(strict, a, 0.0)
        acc = acc + jnp.dot(a.astype(BF16), v_ref[0, ks, heads[h]], preferred_element_type=F32)
        return run + total, acc

    n_heads = len(heads)
    diag_terms = [tile_terms(h, qi, True) for h in range(n_heads)]
    prev_terms = [tile_terms(h, jnp.maximum(qi - 1, 0), False) for h in range(n_heads)]
    first_tile = jnp.where(qi > 0, 0.0, MASKED)
    runs, accs = [], []
    for h in range(n_heads):
        run, acc = tile_update(h, diag_terms[h], jnp.zeros((tile, 1), F32), jnp.zeros((tile, HEAD_DIM), F32), True)
        run, acc = tile_update(h, prev_terms[h], run + first_tile, acc, False)
        runs.append(run)
        accs.append(acc)

    def live(runs):
        return jnp.max(functools.reduce(jnp.maximum, runs)) > SB_EXP_IS_ZERO

    def cond(carry):
        j, go, _, _ = carry
        return jnp.logical_and(j >= 0, go)

    def body(carry):
        j, _, runs, accs = carry
        out = [tile_update(h, tile_terms(h, j, False), runs[h], accs[h], False) for h in range(n_heads)]
        runs, accs = [r for r, _ in out], [a for _, a in out]
        return j - 1, live(runs), runs, accs

    _, _, runs, accs = lax.while_loop(cond, body, (qi - 2, live(runs), runs, accs))
    for h in range(n_heads):
        o_ref[0, :, heads[h]] = accs[h].astype(o_ref.dtype)


def _stick_breaking(qkv, n_heads, tile, heads_per_step, casts):
    grid, in_specs, out_spec, out_shape = _attention_grid(qkv, n_heads, tile, heads_per_step)
    c_in, c_out, c_shapes, c_ops = _cast_specs(casts, grid)
    return pl.pallas_call(
        functools.partial(_sb_kernel, tile=tile, n_casts=len(casts)),
        grid=grid,
        in_specs=in_specs + c_in,
        out_specs=[out_spec, *c_out],
        out_shape=[out_shape, *c_shapes],
        compiler_params=_params("arbitrary", "arbitrary", "arbitrary"),
    )(qkv, qkv, qkv, *c_ops)


def _ffn_kernel(*refs, tm, tiles_per_seq, final, n_chains):
    if final:
        x_ref, g_ref, wu_ref, wg_ref, cw_ref, cb_ref, wd_ref, fg_ref, o_ref, h_ref, ubuf_ref, halo_ref = refs
    else:
        x_ref, g_ref, wu_ref, wg_ref, cw_ref, cb_ref, wd_ref, o_ref, h_ref, ubuf_ref, halo_ref = refs
    i = pl.program_id(0)
    f = pl.program_id(1)
    width = cw_ref.shape[0]

    @pl.when(f == 0)
    def _():
        x = x_ref[...]
        h_ref[...] = _rms(x, g_ref[...]).astype(BF16)
        o_ref[...] = x

        @pl.when(i == 0)
        def _():
            halo_ref[...] = jnp.zeros(halo_ref.shape, F32)

    h = h_ref[...]
    seq_start = (i % tiles_per_seq) == 0
    tf = wu_ref.shape[1]
    chunk = tf // n_chains
    acts = []
    for c in range(n_chains):
        cols = slice(c * chunk, (c + 1) * chunk)
        up = jnp.dot(h, wu_ref[:, cols], preferred_element_type=F32)
        gate = jnp.dot(h, wg_ref[:, cols], preferred_element_type=F32)
        ubuf_ref[0:SUBLANES, cols] = jnp.where(seq_start, 0.0, halo_ref[f, :, cols])
        ubuf_ref[SUBLANES:SUBLANES + tm, cols] = up
        halo_ref[f, :, cols] = up[tm - SUBLANES:tm, :]
        conv = up * cw_ref[width - 1:width, cols] + cb_ref[:, cols]
        for k in range(width - 1):
            conv = conv + ubuf_ref[pl.ds(SUBLANES - (width - 1) + k, tm), cols] * cw_ref[k:k + 1, cols]
        acts.append((conv * jax.nn.sigmoid(conv) * gate).astype(BF16))
    o_ref[...] += jnp.dot(jnp.concatenate(acts, axis=1), wd_ref[...], preferred_element_type=F32)

    if final:
        @pl.when(f == pl.num_programs(1) - 1)
        def _():
            o_ref[...] = _rms(o_ref[...], fg_ref[...])


def _ffn(x, g, w_up, w_gate, conv_w, conv_b, w_down, final_g, t_len, tm, tf, n_chains):
    n, d = x.shape
    d_ff = w_up.shape[1]
    width = conv_w.shape[0]
    final = final_g is not None
    row = lambda i, f: (i, 0)
    in_specs = [pl.BlockSpec((tm, d), row),
                pl.BlockSpec((1, d), lambda i, f: (0, 0)),
                pl.BlockSpec((d, tf), lambda i, f: (0, f)),
                pl.BlockSpec((d, tf), lambda i, f: (0, f)),
                pl.BlockSpec((width, tf), lambda i, f: (0, f)),
                pl.BlockSpec((1, tf), lambda i, f: (0, f)),
                pl.BlockSpec((tf, d), lambda i, f: (f, 0))]
    args = [x, g, w_up, w_gate, conv_w, conv_b, w_down]
    if final:
        in_specs.append(pl.BlockSpec((1, d), lambda i, f: (0, 0)))
        args.append(final_g)
    return pl.pallas_call(
        functools.partial(_ffn_kernel, tm=tm, tiles_per_seq=t_len // tm, final=final, n_chains=n_chains),
        grid=(n // tm, d_ff // tf),
        in_specs=in_specs,
        out_specs=pl.BlockSpec((tm, d), row),
        out_shape=jax.ShapeDtypeStruct((n, d), F32),
        scratch_shapes=[pltpu.VMEM((tm, d), BF16),
                        pltpu.VMEM((SUBLANES + tm, tf), F32),
                        pltpu.VMEM((d_ff // tf, SUBLANES, tf), F32)],
        compiler_params=_params("arbitrary", "arbitrary"),
    )(*args)


def kernel(x, mix_norm, ffn_norm, even_w_in, even_conv_w, even_conv_b, even_ln_g, even_ln_b, even_w_out,
           odd_w_qkv, odd_w_o, ffn_w_up, ffn_w_gate, ffn_conv_w, ffn_conv_b, ffn_w_down, final_norm):
    bsz, t_len, d = x.shape
    n = bsz * t_len
    depth = mix_norm.shape[0]
    xs = x.reshape(n, d)

    def later_weights(layer, steps):
        def rows(r):
            return next(rb for rb in range(2 * SUBLANES, r + 1, 2 * SUBLANES) if r % rb == 0 and r // rb <= steps)
        stacks = [(even_w_out if layer % 2 == 0 else odd_w_o, layer // 2),
                  (ffn_w_up, layer), (ffn_w_gate, layer), (ffn_w_down, layer)]
        if layer + 1 < depth:
            stacks.append((odd_w_qkv if layer % 2 == 0 else even_w_in, (layer + 1) // 2))
        return [(w, idx, rows(w.shape[1])) for w, idx in stacks]

    w_first = None
    for layer in range(depth):
        j = layer // 2
        g = mix_norm[layer][None, :]
        if layer % 2 == 0:
            w_in = even_w_in[j].astype(BF16) if w_first is None else w_first
            ag = _norm_matmul(xs, g, w_in, F32, 1024, 2048, 0, 2 * CONV_CH)
            qkv = _norm_matmul(xs, g, w_in, BF16, 1024, 1024, 2 * CONV_CH).reshape(bsz, t_len, -1)
            y_a = _conformer(ag.reshape(bsz, t_len, -1), even_conv_w[j], even_conv_b[j][None, :],
                             even_ln_g[j][None, :], even_ln_b[j][None, :], 512)
            steps = bsz * (MOBA_HEADS // MOBA_HEADS_PER_STEP) * (t_len // MOBA_BLOCK)
            y_b, w_proj, w_up, w_gate, w_down, *w_next = _moba(
                qkv, MOBA_HEADS, MOBA_BLOCK, MOBA_TOPK, MOBA_KEY_TILE, MOBA_HEADS_PER_STEP, later_weights(layer, steps))
            xs = _matmul_res([y_a.reshape(n, -1), y_b.reshape(n, -1)], w_proj, xs, 1024, 1024)
        else:
            w_qkv = odd_w_qkv[j].astype(BF16) if w_first is None else w_first
            qkv = _norm_matmul(xs, g, w_qkv, BF16, 1024, 2048)
            steps = bsz * (SB_HEADS // SB_HEADS_PER_STEP) * (t_len // SB_TILE)
            y, w_proj, w_up, w_gate, w_down, *w_next = _stick_breaking(
                qkv.reshape(bsz, t_len, -1), SB_HEADS, SB_TILE, SB_HEADS_PER_STEP, later_weights(layer, steps))
            xs = _matmul_res([y.reshape(n, -1)], w_proj, xs, 1024, 1024)
        w_first = w_next[0] if w_next else None
        final_g = final_norm[None, :] if layer == depth - 1 else None
        xs = _ffn(xs, ffn_norm[layer][None, :], w_up, w_gate, ffn_conv_w[layer], ffn_conv_b[layer][None, :],
                  w_down, final_g, t_len, 1024, 512, 2)
    return xs.reshape(bsz, t_len, d)
```

```python
import functools

import jax
import jax.numpy as jnp
from jax import lax
from jax.experimental import pallas as pl
from jax.experimental.pallas import tpu as pltpu

F32 = jnp.float32
BF16 = jnp.bfloat16

HEAD_DIM = 128
CONV_CH = 1024
CONV_WIDTH = 31
MOBA_HEADS = 8
MOBA_BLOCK = 256
MOBA_TOPK = 3
MOBA_KEY_TILE = 1024
MOBA_HEADS_PER_STEP = 4
SB_HEADS = 16
SB_TILE = 256
SB_HEADS_PER_STEP = 8
RMS_EPS = 1e-6
LN_EPS = 1e-5
LANES = 128
SUBLANES = 8
MASKED = -1e30
LOG2_E = 1.4426950408889634
SB_EXP_IS_ZERO = -105.0
VMEM_LIMIT = 58 * 1024 * 1024

_NT = (((1,), (1,)), ((), ()))


def _params(*semantics):
    return pltpu.CompilerParams(dimension_semantics=semantics, vmem_limit_bytes=VMEM_LIMIT)


def _rms(x, g):
    ms = jnp.mean(x * x, axis=-1, keepdims=True)
    return x * lax.rsqrt(ms + RMS_EPS) * g


def _cast_specs(casts, grid):
    def step(idx):
        s = idx[0]
        for extent, i in zip(grid[1:], idx[1:]):
            s = s * extent + i
        return s

    n_steps = functools.reduce(lambda a, b: a * b, grid)
    in_specs, out_specs, out_shapes, operands = [], [], [], []
    for stack, layer, rb in casts:
        _, r, c = stack.shape
        assert r % rb == 0 and r // rb <= n_steps
        last = r // rb - 1
        in_specs.append(pl.BlockSpec((None, rb, c), functools.partial(
            lambda *idx, layer, last: (layer, jnp.minimum(step(idx), last), 0), layer=layer, last=last)))
        out_specs.append(pl.BlockSpec((rb, c), functools.partial(
            lambda *idx, last: (jnp.minimum(step(idx), last), 0), last=last)))
        out_shapes.append(jax.ShapeDtypeStruct((r, c), BF16))
        operands.append(stack)
    return in_specs, out_specs, out_shapes, operands


def _run_casts(src_refs, dst_refs):
    for src_ref, dst_ref in zip(src_refs, dst_refs):
        dst_ref[...] = src_ref[...].astype(BF16)


def _norm_matmul_kernel(x_ref, g_ref, w_ref, o_ref, h_ref):
    @pl.when(pl.program_id(1) == 0)
    def _():
        h_ref[...] = _rms(x_ref[...], g_ref[...]).astype(BF16)

    o_ref[...] = jnp.dot(h_ref[...], w_ref[...], preferred_element_type=F32).astype(o_ref.dtype)


def _norm_matmul(x, g, w, out_dtype, tm, tn, col_start=0, col_stop=None):
    n, d = x.shape
    col_stop = w.shape[1] if col_stop is None else col_stop
    m = col_stop - col_start
    assert col_start % tn == 0 and m % tn == 0 and n % tm == 0
    first = col_start // tn
    return pl.pallas_call(
        _norm_matmul_kernel,
        grid=(n // tm, m // tn),
        in_specs=[pl.BlockSpec((tm, d), lambda i, j: (i, 0)),
                  pl.BlockSpec((1, d), lambda i, j: (0, 0)),
                  pl.BlockSpec((d, tn), lambda i, j: (0, first + j))],
        out_specs=pl.BlockSpec((tm, tn), lambda i, j: (i, j)),
        out_shape=jax.ShapeDtypeStruct((n, m), out_dtype),
        scratch_shapes=[pltpu.VMEM((tm, d), BF16)],
        compiler_params=_params("parallel", "arbitrary"),
    )(x, g, w)


def _matmul_res_kernel(*refs, n_pairs):
    a_refs, w_refs = refs[:n_pairs], refs[n_pairs:2 * n_pairs]
    x_ref, o_ref = refs[2 * n_pairs], refs[2 * n_pairs + 1]
    acc = x_ref[...]
    for a_ref, w_ref in zip(a_refs, w_refs):
        acc = acc + jnp.dot(a_ref[...], w_ref[...], preferred_element_type=F32)
    o_ref[...] = acc


def _matmul_res(a_list, w, x, tm, tn):
    n, m = x.shape
    n_pairs = len(a_list)
    widths = [a.shape[1] for a in a_list]
    assert all(k == widths[0] for k in widths) and sum(widths) == w.shape[0]
    in_specs = ([pl.BlockSpec((tm, k), lambda i, j: (i, 0)) for k in widths]
                + [pl.BlockSpec((k, tn), functools.partial(lambda i, j, p: (p, j), p=p)) for p, k in enumerate(widths)]
                + [pl.BlockSpec((tm, tn), lambda i, j: (i, j))])
    return pl.pallas_call(
        functools.partial(_matmul_res_kernel, n_pairs=n_pairs),
        grid=(n // tm, m // tn),
        in_specs=in_specs,
        out_specs=pl.BlockSpec((tm, tn), lambda i, j: (i, j)),
        out_shape=jax.ShapeDtypeStruct((n, m), F32),
        compiler_params=_params("parallel", "arbitrary"),
    )(*a_list, *([w] * n_pairs), x)


def _conformer_kernel(a_ref, g_ref, w_ref, b_ref, lng_ref, lnb_ref, o_ref, buf_ref, hs_ref, *, tt, halo):
    n_chunks = buf_ref.shape[0]
    width = w_ref.shape[1]

    @pl.when(pl.program_id(1) == 0)
    def _():
        buf_ref[:, 0:halo, :] = jnp.zeros((n_chunks, halo, LANES), F32)

    glu = a_ref[0] * jax.nn.sigmoid(g_ref[0])
    for c in range(n_chunks):
        buf_ref[c, halo:halo + tt, :] = glu[:, c * LANES:(c + 1) * LANES]

    def chunk_body(c, carry):
        acc = jnp.broadcast_to(b_ref[c], (tt, LANES))
        for k in range(width):
            acc = acc + buf_ref[c, pl.ds(halo - (width - 1) + k, tt), :] * w_ref[c, pl.ds(k, 1), :]
        hs_ref[c] = acc
        buf_ref[c, 0:halo, :] = buf_ref[c, tt:tt + halo, :]
        return carry

    lax.fori_loop(0, n_chunks, chunk_body, 0)

    channels = n_chunks * LANES
    total = hs_ref[0]
    for c in range(1, n_chunks):
        total = total + hs_ref[c]
    mu = jnp.sum(total, axis=-1, keepdims=True) * (1.0 / channels)
    sq = jnp.zeros((tt, LANES), F32)
    for c in range(n_chunks):
        d = hs_ref[c] - mu
        sq = sq + d * d
    var = jnp.sum(sq, axis=-1, keepdims=True) * (1.0 / channels)
    inv = lax.rsqrt(var + LN_EPS)
    for c in range(n_chunks):
        y = (hs_ref[c] - mu) * inv * lng_ref[c] + lnb_ref[c]
        o_ref[0, :, c * LANES:(c + 1) * LANES] = (y * jax.nn.sigmoid(y)).astype(o_ref.dtype)


def _conformer(ag, conv_w, conv_b, ln_g, ln_b, tt):
    bsz, t_len, c2 = ag.shape
    ch = c2 // 2
    n_chunks = ch // LANES
    width = conv_w.shape[0]
    halo = -(-(width - 1) // SUBLANES) * SUBLANES
    by_chunk = lambda p: p.reshape(-1, n_chunks, LANES).transpose(1, 0, 2)
    small = lambda rows: pl.BlockSpec((n_chunks, rows, LANES), lambda b, t: (0, 0, 0))
    return pl.pallas_call(
        functools.partial(_conformer_kernel, tt=tt, halo=halo),
        grid=(bsz, t_len // tt),
        in_specs=[pl.BlockSpec((1, tt, ch), lambda b, t: (b, t, 0)),
                  pl.BlockSpec((1, tt, ch), lambda b, t: (b, t, 1)),
                  small(width), small(1), small(1), small(1)],
        out_specs=pl.BlockSpec((1, tt, ch), lambda b, t: (b, t, 0)),
        out_shape=jax.ShapeDtypeStruct((bsz, t_len, ch), BF16),
        scratch_shapes=[pltpu.VMEM((n_chunks, halo + tt, LANES), F32),
                        pltpu.VMEM((n_chunks, tt, LANES), F32)],
        compiler_params=_params("parallel", "arbitrary"),
    )(ag, ag, by_chunk(conv_w), by_chunk(conv_b), by_chunk(ln_g), by_chunk(ln_b))


def _moba_kernel(*refs, blk, topk, key_tile, n_casts):
    q_ref, k_ref, v_ref = refs[:3]
    o_ref = refs[3 + n_casts]
    kaug_ref, kmean_ref = refs[-2:]
    _run_casts(refs[3:3 + n_casts], refs[4 + n_casts:4 + 2 * n_casts])
    qi = pl.program_id(2)
    t_len = k_ref.shape[1]
    nb = t_len // blk
    heads = [slice(h * HEAD_DIM, (h + 1) * HEAD_DIM) for h in range(q_ref.shape[2] // HEAD_DIM)]
    c2 = HEAD_DIM ** -0.5 * LOG2_E

    @pl.when(qi == 0)
    def _():
        row = lax.broadcasted_iota(jnp.int32, (t_len, LANES), 0)
        col = lax.broadcasted_iota(jnp.int32, (t_len, LANES), 1)
        block_id = jnp.where(row // blk == col, 1.0, 0.0).astype(BF16)
        for h, hd in enumerate(heads):
            k = k_ref[0, :, hd]
            kaug_ref[h, :, 0:HEAD_DIM] = k
            kaug_ref[h, :, HEAD_DIM:HEAD_DIM + LANES] = block_id
            kmean = jnp.mean(k.astype(F32).reshape(nb, blk, HEAD_DIM), axis=1)
            kmean_ref[h] = kmean.astype(BF16)

    own = pl.ds(pl.multiple_of(qi * blk, blk), blk)
    r_i = lax.broadcasted_iota(jnp.int32, (blk, blk), 0)
    c_i = lax.broadcasted_iota(jnp.int32, (blk, blk), 1)
    causal = c_i <= r_i
    blk_id = lax.broadcasted_iota(jnp.int32, (nb, blk), 0)
    blk_id_f = blk_id.astype(F32)
    past = blk_id < qi

    def head_start(h):
        q = q_ref[0, :, heads[h]]
        gate = lax.dot_general(kmean_ref[h], q, _NT, preferred_element_type=F32)
        g = jnp.where(past, gate, -jnp.inf)
        allowed = jnp.zeros(gate.shape, jnp.bool_)
        for _ in range(topk):
            best = jnp.max(g, axis=0, keepdims=True)
            first = jnp.min(jnp.where(g == best, blk_id_f, float(nb)), axis=0, keepdims=True)
            pick = blk_id_f == first
            allowed = jnp.logical_or(allowed, jnp.logical_and(pick, past))
            g = jnp.where(pick, -jnp.inf, g)
        penalty_t = jnp.concatenate([jnp.where(allowed, 0.0, MASKED), jnp.zeros((LANES - nb, blk), F32)], axis=0)
        q_aug = jnp.concatenate([q, penalty_t.T.astype(BF16)], axis=1)
        s = lax.dot_general(q, k_ref[0, own, heads[h]], _NT, preferred_element_type=F32)
        s = jnp.where(causal, s, MASKED)
        m = jnp.max(s, axis=1, keepdims=True)
        p = jnp.exp2((s - m) * c2)
        l = jnp.sum(p, axis=1, keepdims=True)
        acc = jnp.dot(p.astype(BF16), v_ref[0, own, heads[h]], preferred_element_type=F32)
        return q_aug, (m, l, acc)

    started = [head_start(h) for h in range(len(heads))]
    q_augs = [qa for qa, _ in started]

    def body(j, carry):
        ks = pl.ds(pl.multiple_of(j * key_tile, key_tile), key_tile)
        out = []
        for h, (m, l, acc) in enumerate(carry):
            s = lax.dot_general(q_augs[h], kaug_ref[h, ks, :], _NT, preferred_element_type=F32)
            m_new = jnp.maximum(m, jnp.max(s, axis=1, keepdims=True))
            alpha = jnp.exp2((m - m_new) * c2)
            p = jnp.exp2((s - m_new) * c2)
            l = alpha * l + jnp.sum(p, axis=1, keepdims=True)
            acc = alpha * acc + jnp.dot(p.astype(BF16), v_ref[0, ks, heads[h]], preferred_element_type=F32)
            out.append((m_new, l, acc))
        return out

    n_tiles = (qi * blk + key_tile - 1) // key_tile
    final = lax.fori_loop(0, n_tiles, body, [st for _, st in started])
    for h, (m, l, acc) in enumerate(final):
        o_ref[0, :, heads[h]] = (acc / l).astype(o_ref.dtype)


def _attention_grid(qkv, n_heads, tile, heads_per_step):
    bsz, t_len, _ = qkv.shape
    assert n_heads % heads_per_step == 0 and t_len % tile == 0
    groups = n_heads // heads_per_step
    cols = heads_per_step * HEAD_DIM
    q_spec = pl.BlockSpec((1, tile, cols), lambda b, h, i: (b, i, h))
    k_spec = pl.BlockSpec((1, t_len, cols), lambda b, h, i: (b, 0, groups + h))
    v_spec = pl.BlockSpec((1, t_len, cols), lambda b, h, i: (b, 0, 2 * groups + h))
    out_shape = jax.ShapeDtypeStruct((bsz, t_len, n_heads * HEAD_DIM), BF16)
    return (bsz, groups, t_len // tile), [q_spec, k_spec, v_spec], q_spec, out_shape


def _moba(qkv, n_heads, blk, topk, key_tile, heads_per_step, casts):
    t_len = qkv.shape[1]
    assert t_len % key_tile == 0 and key_tile % blk == 0
    assert (t_len // blk) % SUBLANES == 0 and t_len // blk <= LANES
    grid, in_specs, out_spec, out_shape = _attention_grid(qkv, n_heads, blk, heads_per_step)
    c_in, c_out, c_shapes, c_ops = _cast_specs(casts, grid)
    return pl.pallas_call(
        functools.partial(_moba_kernel, blk=blk, topk=topk, key_tile=key_tile, n_casts=len(casts)),
        grid=grid,
        in_specs=in_specs + c_in,
        out_specs=[out_spec, *c_out],
        out_shape=[out_shape, *c_shapes],
        scratch_shapes=[pltpu.VMEM((heads_per_step, t_len, HEAD_DIM + LANES), BF16),
                        pltpu.VMEM((heads_per_step, t_len // blk, HEAD_DIM), BF16)],
        compiler_params=_params("arbitrary", "arbitrary", "arbitrary"),
    )(qkv, qkv, qkv, *c_ops)


def _sb_kernel(*refs, tile, n_casts):
    q_ref, k_ref, v_ref = refs[:3]
    o_ref = refs[3 + n_casts]
    _run_casts(refs[3:3 + n_casts], refs[4 + n_casts:])
    qi = pl.program_id(2)
    scale = HEAD_DIM ** -0.5
    heads = [slice(h * HEAD_DIM, (h + 1) * HEAD_DIM) for h in range(q_ref.shape[2] // HEAD_DIM)]
    neg_q = [-q_ref[0, :, hd] for hd in heads]
    r_i = lax.broadcasted_iota(jnp.int32, (tile, tile), 0)
    c_i = lax.broadcasted_iota(jnp.int32, (tile, tile), 1)
    later = jnp.where(r_i > c_i, 1.0, 0.0).astype(BF16)
    later2 = jnp.concatenate([later, later], axis=0)
    strict = c_i < r_i

    def tile_terms(h, j, diagonal):
        ks = pl.ds(pl.multiple_of(j * tile, tile), tile)
        nz = lax.dot_general(neg_q[h], k_ref[0, ks, heads[h]], _NT, preferred_element_type=F32) * scale
        log_1m = jnp.minimum(nz, 0.0) - jnp.log(1.0 + jnp.exp2(jnp.abs(nz) * -LOG2_E))
        if diagonal:
            log_1m = jnp.where(strict, log_1m, 0.0)
        hi = log_1m.astype(BF16)
        lo = (log_1m - hi.astype(F32)).astype(BF16)
        after = jnp.dot(jnp.concatenate([hi, lo], axis=1), later2, preferred_element_type=F32)
        return ks, log_1m - nz + after, jnp.sum(log_1m, axis=1, keepdims=True)

    def tile_update(h, terms, run, acc, diagonal):
        ks, log_w, total = terms
        a = jnp.exp(log_w + run)
        if diagonal:
            a = jnp.where(strict, a, 0.0)
        acc = acc + jnp.dot(a.astype(BF16), v_ref[0, ks, heads[h]], preferred_element_type=F32)
        return run + total, acc

    n_heads = len(heads)
    diag_terms = [tile_terms(h, qi, True) for h in range(n_heads)]
    prev_terms = [tile_terms(h, jnp.maximum(qi - 1, 0), False) for h in range(n_heads)]
    first_tile = jnp.where(qi > 0, 0.0, MASKED)
    runs, accs = [], []
    for h in range(n_heads):
        run, acc = tile_update(h, diag_terms[h], jnp.zeros((tile, 1), F32), jnp.zeros((tile, HEAD_DIM), F32), True)
        run, acc = tile_update(h, prev_terms[h], run + first_tile, acc, False)
        runs.append(run)
        accs.append(acc)

    def live(runs):
        return jnp.max(functools.reduce(jnp.maximum, runs)) > SB_EXP_IS_ZERO

    def cond(carry):
        j, go, _, _ = carry
        return jnp.logical_and(j >= 0, go)

    def body(carry):
        j, _, runs, accs = carry
        out = [tile_update(h, tile_terms(h, j, False), runs[h], accs[h], False) for h in range(n_heads)]
        runs, accs = [r for r, _ in out], [a for _, a in out]
        return j - 1, live(runs), runs, accs

    _, _, runs, accs = lax.while_loop(cond, body, (qi - 2, live(runs), runs, accs))
    for h in range(n_heads):
        o_ref[0, :, heads[h]] = accs[h].astype(o_ref.dtype)


def _stick_breaking(qkv, n_heads, tile, heads_per_step, casts):
    grid, in_specs, out_spec, out_shape = _attention_grid(qkv, n_heads, tile, heads_per_step)
    c_in, c_out, c_shapes, c_ops = _cast_specs(casts, grid)
    return pl.pallas_call(
        functools.partial(_sb_kernel, tile=tile, n_casts=len(casts)),
        grid=grid,
        in_specs=in_specs + c_in,
        out_specs=[out_spec, *c_out],
        out_shape=[out_shape, *c_shapes],
        compiler_params=_params("arbitrary", "arbitrary", "arbitrary"),
    )(qkv, qkv, qkv, *c_ops)


def _ffn_kernel(*refs, tm, tiles_per_seq, final, n_chains):
    if final:
        x_ref, g_ref, wu_ref, wg_ref, cw_ref, cb_ref, wd_ref, fg_ref, o_ref, h_ref, ubuf_ref, halo_ref = refs
    else:
        x_ref, g_ref, wu_ref, wg_ref, cw_ref, cb_ref, wd_ref, o_ref, h_ref, ubuf_ref, halo_ref = refs
    i = pl.program_id(0)
    f = pl.program_id(1)
    width = cw_ref.shape[0]

    @pl.when(f == 0)
    def _():
        x = x_ref[...]
        h_ref[...] = _rms(x, g_ref[...]).astype(BF16)
        o_ref[...] = x

        @pl.when(i == 0)
        def _():
            halo_ref[...] = jnp.zeros(halo_ref.shape, F32)

    h = h_ref[...]
    seq_start = (i % tiles_per_seq) == 0
    tf = wu_ref.shape[1]
    chunk = tf // n_chains
    acts = []
    for c in range(n_chains):
        cols = slice(c * chunk, (c + 1) * chunk)
        up = jnp.dot(h, wu_ref[:, cols], preferred_element_type=F32)
        gate = jnp.dot(h, wg_ref[:, cols], preferred_element_type=F32)
        ubuf_ref[0:SUBLANES, cols] = jnp.where(seq_start, 0.0, halo_ref[f, :, cols])
        ubuf_ref[SUBLANES:SUBLANES + tm, cols] = up
        halo_ref[f, :, cols] = up[tm - SUBLANES:tm, :]
        conv = up * cw_ref[width - 1:width, cols] + cb_ref[:, cols]
        for k in range(width - 1):
            conv = conv + ubuf_ref[pl.ds(SUBLANES - (width - 1) + k, tm), cols] * cw_ref[k:k + 1, cols]
        acts.append((conv * jax.nn.sigmoid(conv) * gate).astype(BF16))
    o_ref[...] += jnp.dot(jnp.concatenate(acts, axis=1), wd_ref[...], preferred_element_type=F32)

    if final:
        @pl.when(f == pl.num_programs(1) - 1)
        def _():
            o_ref[...] = _rms(o_ref[...], fg_ref[...])


def _ffn(x, g, w_up, w_gate, conv_w, conv_b, w_down, final_g, t_len, tm, tf, n_chains):
    n, d = x.shape
    d_ff = w_up.shape[1]
    width = conv_w.shape[0]
    final = final_g is not None
    row = lambda i, f: (i, 0)
    in_specs = [pl.BlockSpec((tm, d), row),
                pl.BlockSpec((1, d), lambda i, f: (0, 0)),
                pl.BlockSpec((d, tf), lambda i, f: (0, f)),
                pl.BlockSpec((d, tf), lambda i, f: (0, f)),
                pl.BlockSpec((width, tf), lambda i, f: (0, f)),
                pl.BlockSpec((1, tf), lambda i, f: (0, f)),
                pl.BlockSpec((tf, d), lambda i, f: (f, 0))]
    args = [x, g, w_up, w_gate, conv_w, conv_b, w_down]
    if final:
        in_specs.append(pl.BlockSpec((1, d), lambda i, f: (0, 0)))
        args.append(final_g)
    return pl.pallas_call(
        functools.partial(_ffn_kernel, tm=tm, tiles_per_seq=t_len // tm, final=final, n_chains=n_chains),
        grid=(n // tm, d_ff // tf),
        in_specs=in_specs,
        out_specs=pl.BlockSpec((tm, d), row),
        out_shape=jax.ShapeDtypeStruct((n, d), F32),
        scratch_shapes=[pltpu.VMEM((tm, d), BF16),
                        pltpu.VMEM((SUBLANES + tm, tf), F32),
                        pltpu.VMEM((d_ff // tf, SUBLANES, tf), F32)],
        compiler_params=_params("arbitrary", "arbitrary"),
    )(*args)


def kernel(x, mix_norm, ffn_norm, even_w_in, even_conv_w, even_conv_b, even_ln_g, even_ln_b, even_w_out,
           odd_w_qkv, odd_w_o, ffn_w_up, ffn_w_gate, ffn_conv_w, ffn_conv_b, ffn_w_down, final_norm):
    bsz, t_len, d = x.shape
    n = bsz * t_len
    depth = mix_norm.shape[0]
    xs = x.reshape(n, d)

    def later_weights(layer, steps):
        def rows(r):
            return next(rb for rb in range(2 * SUBLANES, r + 1, 2 * SUBLANES) if r % rb == 0 and r // rb <= steps)
        stacks = [(even_w_out if layer % 2 == 0 else odd_w_o, layer // 2),
                  (ffn_w_up, layer), (ffn_w_gate, layer), (ffn_w_down, layer)]
        if layer + 1 < depth:
            stacks.append((odd_w_qkv if layer % 2 == 0 else even_w_in, (layer + 1) // 2))
        return [(w, idx, rows(w.shape[1])) for w, idx in stacks]

    w_first = None
    for layer in range(depth):
        j = layer // 2
        g = mix_norm[layer][None, :]
        if layer % 2 == 0:
            w_glu, w_att = ((even_w_in[j][:, :2 * CONV_CH].astype(BF16), even_w_in[j][:, 2 * CONV_CH:].astype(BF16))
                            if w_first is None else (w_first[:, :2 * CONV_CH], w_first[:, 2 * CONV_CH:]))
            ag = _norm_matmul(xs, g, w_glu, F32, 1024, 2048)
            qkv = _norm_matmul(xs, g, w_att, BF16, 1024, w_att.shape[1] // 2).reshape(bsz, t_len, -1)
            y_a = _conformer(ag.reshape(bsz, t_len, -1), even_conv_w[j], even_conv_b[j][None, :],
                             even_ln_g[j][None, :], even_ln_b[j][None, :], 512)
            steps = bsz * (MOBA_HEADS // MOBA_HEADS_PER_STEP) * (t_len // MOBA_BLOCK)
            y_b, w_proj, w_up, w_gate, w_down, *w_next = _moba(
                qkv, MOBA_HEADS, MOBA_BLOCK, MOBA_TOPK, MOBA_KEY_TILE, MOBA_HEADS_PER_STEP, later_weights(layer, steps))
            xs = _matmul_res([y_a.reshape(n, -1), y_b.reshape(n, -1)], w_proj, xs, 1024, 1024)
        else:
            w_qkv = odd_w_qkv[j].astype(BF16) if w_first is None else w_first
            qkv = _norm_matmul(xs, g, w_qkv, BF16, 1024, 2048)
            steps = bsz * (SB_HEADS // SB_HEADS_PER_STEP) * (t_len // SB_TILE)
            y, w_proj, w_up, w_gate, w_down, *w_next = _stick_breaking(
                qkv.reshape(bsz, t_len, -1), SB_HEADS, SB_TILE, SB_HEADS_PER_STEP, later_weights(layer, steps))
            xs = _matmul_res([y.reshape(n, -1)], w_proj, xs, 1024, 1024)
        w_first = w_next[0] if w_next else None
        final_g = final_norm[None, :] if layer == depth - 1 else None
        xs = _ffn(xs, ffn_norm[layer][None, :], w_up, w_gate, ffn_conv_w[layer], ffn_conv_b[layer][None, :],
                  w_down, final_g, t_len, 1024, 512, 2)
    return xs.reshape(bsz, t_len, d)
```
